```python
import math
import jax, jax.numpy as jnp
from jax import lax
import numpy as np

D_MODEL = 1024
BATCH = 1
SEQ = 16384
DEPTH = 2
DEC_BATCH = 32
DEC_SEQ = 4
PAST_LEN = 16384
PAGE_SIZE = 128

N_MIXERS = 2
N_POOL_LAYERS = (DEPTH + 1) // 2
N_ATTN_LAYERS = DEPTH // 2
POOL_WINDOWS = (2, 4, 8, 16)
N_POOL_GROUPS = len(POOL_WINDOWS)
POOL_GROUP = D_MODEL // N_POOL_GROUPS
POOL_BUF = max(POOL_WINDOWS) - 1
N_HEADS = 8
HEAD_QK = 64
HEAD_V = 2 * HEAD_QK
KEY_DIM = 2 * HEAD_QK
Q_BLOCK = 128
N_EXPERTS = 16
N_GROUPS = 4
EXPERTS_PER_GROUP = N_EXPERTS // N_GROUPS
TOP_K = 2
D_EXPERT = D_MODEL // 2
RMS_EPS = 1e-6

kernel_name = "hybrid_pool_diffattn_grouped_moe_step"


def rms_norm(x, g):
    xf = x.astype(jnp.float32)
    y = xf * lax.rsqrt(jnp.mean(xf * xf, axis=-1, keepdims=True) + RMS_EPS)
    return (y * g.astype(jnp.float32)).astype(x.dtype)


def modulate(x, g, shift, scale):
    return rms_norm(x, g) * (1 + scale[:, None, :]) + shift[:, None, :]


def alibi_slopes():
    return 2.0 ** (-8.0 * jnp.arange(1, N_HEADS + 1, dtype=jnp.float32) / N_HEADS)


def lambda_init(layer):
    return 0.8 - 0.6 * math.exp(-0.3 * layer)


def pool_mix(u, w_grp, scale):
    B, L, D = u.shape
    uf = u.astype(jnp.float32)
    cs = jnp.concatenate([jnp.zeros((B, 1, D), jnp.float32), lax.cumsum(uf, axis=1)], axis=1)
    pos = jnp.arange(L)
    diffs = []
    for g, w in enumerate(POOL_WINDOWS):
        sl = slice(g * POOL_GROUP, (g + 1) * POOL_GROUP)
        lo = jnp.maximum(pos + 1 - w, 0)
        cnt = (pos + 1 - lo).astype(jnp.float32)[None, :, None]
        csg = cs[:, :, sl]
        mean = (csg[:, 1:] - csg[:, lo]) / cnt
        diffs.append(mean - uf[:, :, sl])
    d = jnp.stack(diffs, axis=2).astype(u.dtype)
    y = jnp.einsum('blgc,gcd->blgd', d, w_grp)
    return y.reshape(B, L, D) * scale


def attn_project(h, w_qkv, q_norm_g, k_norm_g):
    B, L, _ = h.shape
    q, k, v = jnp.split(h @ w_qkv, 3, axis=-1)
    q = rms_norm(q.reshape(B, L, N_HEADS, 2, HEAD_QK), q_norm_g)
    k = rms_norm(k.reshape(B, L, N_HEADS, 2, HEAD_QK), k_norm_g)
    v = v.reshape(B, L, N_HEADS, HEAD_V)
    return q, k, v


def diff_lambda(lq1, lk1, lq2, lk2, lam0):
    f32 = jnp.float32
    return (jnp.exp(jnp.sum(lq1.astype(f32) * lk1.astype(f32)))
            - jnp.exp(jnp.sum(lq2.astype(f32) * lk2.astype(f32))) + lam0)


def diff_attend(q, k, v, q_pos, k_pos, slopes, lam):
    s = jnp.einsum('bqhcd,bkhcd->bhcqk', q, k, preferred_element_type=jnp.float32) * (HEAD_QK ** -0.5)
    dist = (q_pos[:, None] - k_pos[None, :]).astype(jnp.float32)
    s = s - slopes[None, :, None, None, None] * dist
    s = jnp.where(dist >= 0, s, -jnp.inf)
    p = jax.nn.softmax(s, axis=-1).astype(v.dtype)
    o = jnp.einsum('bhcqk,bkhd->bqhcd', p, v, preferred_element_type=jnp.float32)
    return o[..., 0, :] - lam * o[..., 1, :]


def attn_output(o, lam0, subln_g, w_o):
    B, L = o.shape[:2]
    o = rms_norm(o, subln_g) * (1.0 - lam0)
    return o.reshape(B, L, N_HEADS * HEAD_V).astype(w_o.dtype) @ w_o


def moe(h, w_router, router_bias, w_gate, w_up, w_down):
    shp = h.shape
    t = h.reshape(-1, shp[-1])
    logits = jnp.einsum('td,de->te', t, w_router, preferred_element_type=jnp.float32)
    probs = jax.nn.softmax(logits, axis=-1)
    sel = probs + router_bias.astype(jnp.float32)
    grp_score = lax.top_k(sel.reshape(-1, N_GROUPS, EXPERTS_PER_GROUP), TOP_K)[0].sum(-1)
    grp = jnp.argmax(grp_score, axis=-1)
    in_grp = (jnp.arange(N_EXPERTS) // EXPERTS_PER_GROUP)[None, :] == grp[:, None]
    _, idx = lax.top_k(jnp.where(in_grp, sel, -jnp.inf), TOP_K)
    wts = jnp.take_along_axis(probs, idx, axis=-1)
    wts = wts / jnp.sum(wts, axis=-1, keepdims=True)
    gates = jnp.sum(jax.nn.one_hot(idx, N_EXPERTS, dtype=jnp.float32) * wts[..., None], axis=1).astype(t.dtype)
    out = jnp.zeros_like(t)
    for e in range(N_EXPERTS):
        a = jax.nn.silu(t @ w_gate[e]) * (t @ w_up[e])
        out = out + gates[:, e:e + 1] * (a @ w_down[e])
    return out.reshape(shp)


def setup_inputs(seed: int = 0) -> dict:
    key = jax.random.key(seed)
    ks = jax.random.split(key, 32)
    f32 = jnp.float32
    n_pages = PAST_LEN // PAGE_SIZE
    n_used = DEC_BATCH * n_pages
    n_phys = n_used + max(1, n_used // 4)
    nrm = lambda k, shape, s: jax.random.normal(k, shape, f32) * s
    page_table = jax.random.permutation(ks[5], n_phys)[:n_used].reshape(DEC_BATCH, n_pages).astype(jnp.int32)
    return {
        "x_prompt": nrm(ks[0], (BATCH, SEQ, D_MODEL), 1.0),
        "x_sample": nrm(ks[1], (DEC_BATCH, DEC_SEQ, D_MODEL), 1.0),
        "state_pool": nrm(ks[2], (N_POOL_LAYERS, DEC_BATCH, POOL_BUF, D_MODEL), 1.0),
        "cache_k": nrm(ks[3], (N_ATTN_LAYERS, n_phys, PAGE_SIZE, N_HEADS, KEY_DIM), 1.0),
        "cache_v": nrm(ks[4], (N_ATTN_LAYERS, n_phys, PAGE_SIZE, N_HEADS, HEAD_V), 1.0),
        "page_table": page_table,
        "c_prompt": nrm(ks[6], (BATCH, D_MODEL), 1.0),
        "c_sample": nrm(ks[7], (DEC_BATCH, D_MODEL), 1.0),
        "w_mod": nrm(ks[8], (DEPTH, D_MODEL, 6 * D_MODEL), 0.5 * D_MODEL ** -0.5),
        "b_mod": nrm(ks[9], (DEPTH, 6 * D_MODEL), 0.02),
        "norm_mix_g": 1.0 + nrm(ks[10], (DEPTH, D_MODEL), 0.05),
        "norm_ffn_g": 1.0 + nrm(ks[11], (DEPTH, D_MODEL), 0.05),
        "pool_w": nrm(ks[12], (N_POOL_LAYERS, N_POOL_GROUPS, POOL_GROUP, POOL_GROUP), POOL_GROUP ** -0.5),
        "pool_scale": 1.0 + nrm(ks[13], (N_POOL_LAYERS, D_MODEL), 0.1),
        "w_qkv": nrm(ks[14], (N_ATTN_LAYERS, D_MODEL, 3 * D_MODEL), D_MODEL ** -0.5),
        "q_norm_g": 1.0 + nrm(ks[15], (N_ATTN_LAYERS, 2, HEAD_QK), 0.05),
        "k_norm_g": 1.0 + nrm(ks[16], (N_ATTN_LAYERS, 2, HEAD_QK), 0.05),
        "lambda_q1": nrm(ks[17], (N_ATTN_LAYERS, HEAD_QK), 0.1),
        "lambda_k1": nrm(ks[18], (N_ATTN_LAYERS, HEAD_QK), 0.1),
        "lambda_q2": nrm(ks[19], (N_ATTN_LAYERS, HEAD_QK), 0.1),
        "lambda_k2": nrm(ks[20], (N_ATTN_LAYERS, HEAD_QK), 0.1),
        "subln_g": 1.0 + nrm(ks[21], (N_ATTN_LAYERS, HEAD_V), 0.05),
        "w_o": nrm(ks[22], (N_ATTN_LAYERS, N_HEADS * HEAD_V, D_MODEL), (N_HEADS * HEAD_V) ** -0.5),
        "w_router": nrm(ks[23], (D_MODEL, N_EXPERTS), D_MODEL ** -0.5),
        "router_bias": nrm(ks[24], (N_EXPERTS,), 0.01),
        "w_gate": nrm(ks[25], (DEPTH, N_EXPERTS, D_MODEL, D_EXPERT), D_MODEL ** -0.5),
        "w_up": nrm(ks[26], (DEPTH, N_EXPERTS, D_MODEL, D_EXPERT), D_MODEL ** -0.5),
        "w_down": nrm(ks[27], (DEPTH, N_EXPERTS, D_EXPERT, D_MODEL), D_EXPERT ** -0.5),
    }


def reference(x_prompt, x_sample, state_pool, cache_k, cache_v, page_table, c_prompt, c_sample,
              w_mod, b_mod, norm_mix_g, norm_ffn_g, pool_w, pool_scale, w_qkv, q_norm_g, k_norm_g,
              lambda_q1, lambda_k1, lambda_q2, lambda_k2, subln_g, w_o, w_router, router_bias,
              w_gate, w_up, w_down):
    slopes = alibi_slopes()
    xp, xs = x_prompt, x_sample
    Bp, L, D = xp.shape
    Bs, ds, _ = xs.shape
    n_pages = page_table.shape[1]
    page = cache_k.shape[2]
    past = n_pages * page
    pool_p, pool_s, kp_l, vp_l, ks_l, vs_l = [], [], [], [], [], []
    for i in range(DEPTH):
        j = i // N_MIXERS
        sh1p, sc1p, g1p, sh2p, sc2p, g2p = jnp.split(jax.nn.silu(c_prompt) @ w_mod[i] + b_mod[i], 6, axis=-1)
        sh1s, sc1s, g1s, sh2s, sc2s, g2s = jnp.split(jax.nn.silu(c_sample) @ w_mod[i] + b_mod[i], 6, axis=-1)
        hp = modulate(xp, norm_mix_g[i], sh1p, sc1p)
        hs = modulate(xs, norm_mix_g[i], sh1s, sc1s)
        if i % N_MIXERS == 0:
            op = pool_mix(hp, pool_w[j], pool_scale[j])
            pool_p.append(hp[:, -POOL_BUF:])
            u = jnp.concatenate([state_pool[j].astype(hs.dtype), hs], axis=1)
            os_ = pool_mix(u, pool_w[j], pool_scale[j])[:, -ds:]
            pool_s.append(u[:, -POOL_BUF:])
        else:
            lam0 = lambda_init(i)
            lam = diff_lambda(lambda_q1[j], lambda_k1[j], lambda_q2[j], lambda_k2[j], lam0)
            qp, kp, vp = attn_project(hp, w_qkv[j], q_norm_g[j], k_norm_g[j])
            k_pos = jnp.arange(L)

            def q_block(b, qp=qp, kp=kp, vp=vp, k_pos=k_pos, lam=lam):
                s0 = b * Q_BLOCK
                qb = lax.dynamic_slice_in_dim(qp, s0, Q_BLOCK, axis=1)
                return diff_attend(qb, kp, vp, s0 + jnp.arange(Q_BLOCK), k_pos, slopes, lam)

            o_p = lax.map(q_block, jnp.arange(L // Q_BLOCK))
            o_p = jnp.moveaxis(o_p, 0, 1).reshape(Bp, L, N_HEADS, HEAD_V)
            op = attn_output(o_p, lam0, subln_g[j], w_o[j])
            kp_l.append(kp.reshape(Bp, L // page, page, N_HEADS, KEY_DIM))
            vp_l.append(vp.reshape(Bp, L // page, page, N_HEADS, HEAD_V))

            q_s, k_s, v_s = attn_project(hs, w_qkv[j], q_norm_g[j], k_norm_g[j])

            def one_seq(args, j=j, lam=lam):
                qq, kk, vv, pages = args
                k_past = cache_k[j, pages].reshape(past, N_HEADS, 2, HEAD_QK).astype(kk.dtype)
                v_past = cache_v[j, pages].reshape(past, N_HEADS, HEAD_V).astype(vv.dtype)
                k_all = jnp.concatenate([k_past, kk], axis=0)
                v_all = jnp.concatenate([v_past, vv], axis=0)
                q_pos = past + jnp.arange(ds)
                k_pos_s = jnp.arange(past + ds)
                return diff_attend(qq[None], k_all[None], v_all[None], q_pos, k_pos_s, slopes, lam)[0]

            o_s = lax.map(one_seq, (q_s, k_s, v_s, page_table))
            os_ = attn_output(o_s, lam0, subln_g[j], w_o[j])
            ks_l.append(k_s.reshape(Bs, ds, N_HEADS, KEY_DIM))
            vs_l.append(v_s)
        xp = xp + g1p[:, None, :] * op
        xs = xs + g1s[:, None, :] * os_
        hp = modulate(xp, norm_ffn_g[i], sh2p, sc2p)
        hs = modulate(xs, norm_ffn_g[i], sh2s, sc2s)
        xp = xp + g2p[:, None, :] * moe(hp, w_router, router_bias, w_gate[i], w_up[i], w_down[i])
        xs = xs + g2s[:, None, :] * moe(hs, w_router, router_bias, w_gate[i], w_up[i], w_down[i])
    return (xp, xs, jnp.stack(pool_p), jnp.stack(pool_s), jnp.stack(kp_l), jnp.stack(vp_l),
            jnp.stack(ks_l), jnp.stack(vs_l))
```

```python
import functools
import math

import jax
import jax.numpy as jnp
from jax import lax
from jax.experimental import pallas as pl
from jax.experimental.pallas import tpu as pltpu

F32 = jnp.float32
BF16 = jnp.bfloat16
HIGHEST = lax.Precision.HIGHEST

D_MODEL = 1024
DEPTH = 2
POOL_WINDOWS = (2, 4, 8, 16)
POOL_GROUP = D_MODEL // len(POOL_WINDOWS)
POOL_BUF = max(POOL_WINDOWS) - 1
HALO = POOL_BUF + 1
N_HEADS = 8
HEAD_QK = 64
HEAD_V = 2 * HEAD_QK
KEY_DIM = 2 * HEAD_QK
N_EXPERTS = 16
EXPERTS_PER_GROUP = 4
N_GROUPS = N_EXPERTS // EXPERTS_PER_GROUP
D_EXPERT = D_MODEL // 2
RMS_EPS = 1e-6
LOG2E = 1.4426950408889634
LANES = 128
NEG_BIG = -1e30

ROW_TILE = 512
MOE_ROW_TILE = 1024
ATTN_BQ = 256
ATTN_BK = ROW_TILE
PAGES_PER_STEP = 8
VMEM_LIMIT = 56 * 1024 * 1024


def _cparams(*sem):
    return pltpu.CompilerParams(dimension_semantics=sem, vmem_limit_bytes=VMEM_LIMIT)


def _lambda_init(layer):
    return 0.8 - 0.6 * math.exp(-0.3 * layer)


def _rms(x, g):
    ms = jnp.mean(x * x, axis=-1, keepdims=True)
    return x * lax.rsqrt(ms + RMS_EPS) * g


def _modulate(x, g, shift, scale):
    return _rms(x, g) * (1.0 + scale) + shift


def _split_mod(mod):
    return [mod[:, k * D_MODEL:(k + 1) * D_MODEL] for k in range(6)]


def _route_gates(h, wrT_ref, rb_ref, gT_ref):
    logits = lax.dot_general(wrT_ref[...], h, (((1,), (1,)), ((), ())),
                             precision=HIGHEST, preferred_element_type=F32)
    mx = jnp.max(logits, axis=0, keepdims=True)
    ex = jnp.exp(logits - mx)
    probs = ex / jnp.sum(ex, axis=0, keepdims=True)
    sel = probs + rb_ref[...]
    srow = [sel[e:e + 1] for e in range(N_EXPERTS)]
    prow = [probs[e:e + 1] for e in range(N_EXPERTS)]

    gscore = []
    for g in range(N_GROUPS):
        r = srow[g * EXPERTS_PER_GROUP:(g + 1) * EXPERTS_PER_GROUP]
        best = None
        for a in range(EXPERTS_PER_GROUP):
            for b in range(a + 1, EXPERTS_PER_GROUP):
                s = r[a] + r[b]
                best = s if best is None else jnp.maximum(best, s)
        gscore.append(best)
    gbest = functools.reduce(jnp.maximum, gscore)
    in_grp, taken = [], None
    for g in range(N_GROUPS):
        hit = gscore[g] == gbest
        if taken is not None:
            hit = jnp.logical_and(hit, jnp.logical_not(taken))
        taken = hit if taken is None else jnp.logical_or(taken, hit)
        in_grp.append(hit)

    def pick(rows, k):
        out = rows[(N_GROUPS - 1) * EXPERTS_PER_GROUP + k]
        for g in range(N_GROUPS - 2, -1, -1):
            out = jnp.where(in_grp[g], rows[g * EXPERTS_PER_GROUP + k], out)
        return out

    v = [pick(srow, k) for k in range(EXPERTS_PER_GROUP)]
    p = [pick(prow, k) for k in range(EXPERTS_PER_GROUP)]

    def first_hits(vals, target):
        hits, seen = [], None
        for x in vals:
            hit = x == target
            if seen is not None:
                hit = jnp.logical_and(hit, jnp.logical_not(seen))
            seen = hit if seen is None else jnp.logical_or(seen, hit)
            hits.append(hit)
        return hits

    top1 = first_hits(v, functools.reduce(jnp.maximum, v))
    rest = [jnp.where(top1[k], -jnp.inf, v[k]) for k in range(EXPERTS_PER_GROUP)]
    top2 = first_hits(rest, functools.reduce(jnp.maximum, rest))
    p1 = functools.reduce(jnp.add, [jnp.where(top1[k], p[k], 0.0) for k in range(EXPERTS_PER_GROUP)])
    p2 = functools.reduce(jnp.add, [jnp.where(top2[k], p[k], 0.0) for k in range(EXPERTS_PER_GROUP)])
    den = p1 + p2
    w1, w2 = p1 / den, p2 / den
    for g in range(N_GROUPS):
        for k in range(EXPERTS_PER_GROUP):
            val = jnp.where(top1[k], w1, jnp.where(top2[k], w2, 0.0))
            e = g * EXPERTS_PER_GROUP + k
            gT_ref[e:e + 1, :] = jnp.where(in_grp[g], val, 0.0)


def _ffn_prologue(x1, mod6, gffn_ref, wrT_ref, rb_ref, x1_ref, h2_ref, gT_ref):
    x1_ref[...] = x1
    h2 = _modulate(x1, gffn_ref[...], mod6[3], mod6[4])
    h2_ref[...] = h2.astype(BF16)
    _route_gates(h2, wrT_ref, rb_ref, gT_ref)


def _mod_kernel(c_ref, w_ref, b_ref, o_ref):
    c = c_ref[...]
    a = c * jax.nn.sigmoid(c)
    o_ref[0] = jnp.dot(a, w_ref[0], precision=HIGHEST, preferred_element_type=F32) + b_ref[0]


def _mod_call(c_all, w_mod, b_mod):
    rows = c_all.shape[0]
    tn = 1536
    return pl.pallas_call(
        _mod_kernel,
        grid=(DEPTH, 6 * D_MODEL // tn),
        in_specs=[pl.BlockSpec((rows, D_MODEL), lambda i, j: (0, 0)),
                  pl.BlockSpec((1, D_MODEL, tn), lambda i, j: (i, 0, j)),
                  pl.BlockSpec((1, 1, tn), lambda i, j: (i, 0, j))],
        out_specs=pl.BlockSpec((1, rows, tn), lambda i, j: (i, 0, j)),
        out_shape=jax.ShapeDtypeStruct((DEPTH, rows, 6 * D_MODEL), F32),
        compiler_params=_cparams("arbitrary", "arbitrary"),
        name="adaln_mod",
    )(c_all, w_mod, b_mod.reshape(DEPTH, 1, 6 * D_MODEL))


def _pool_prompt_kernel(x_ref, mod_ref, gmix_ref, gffn_ref, pw_ref, ps_ref, wrT_ref, rb_ref,
                        x1_ref, h2_ref, gT_ref, pst_ref, ext_ref, *, tm):
    i = pl.program_id(0)

    @pl.when(i == 0)
    def _():
        ext_ref[0:HALO, :] = jnp.zeros((HALO, D_MODEL), F32)

    x = x_ref[...]
    mod6 = _split_mod(mod_ref[...])
    ext_ref[HALO:HALO + tm, :] = _modulate(x, gmix_ref[...], mod6[0], mod6[1])
    pos = i * tm + lax.broadcasted_iota(jnp.int32, (tm, 1), 0)
    ys = []
    for g, w in enumerate(POOL_WINDOWS):
        c0 = g * POOL_GROUP
        cur = ext_ref[HALO:HALO + tm, c0:c0 + POOL_GROUP]
        acc = cur
        for k in range(1, w):
            acc = acc + ext_ref[HALO - k:HALO - k + tm, c0:c0 + POOL_GROUP]
        cnt = jnp.minimum(pos + 1, w).astype(F32)
        d = acc / cnt - cur
        ys.append(jnp.dot(d.astype(BF16), pw_ref[g], preferred_element_type=F32))
    y = jnp.concatenate(ys, axis=1) * ps_ref[...]
    tail = ext_ref[tm:tm + HALO, :]
    ext_ref[0:HALO, :] = tail
    pst_ref[...] = tail
    _ffn_prologue(x + mod6[2] * y, mod6, gffn_ref, wrT_ref, rb_ref, x1_ref, h2_ref, gT_ref)


def _pool_prompt_call(x, mod, gmix, gffn, pool_w, pool_scale, wrT, rb):
    t = x.shape[0]
    tm = ROW_TILE
    const = lambda shape: pl.BlockSpec(shape, lambda i: (0,) * len(shape))
    return pl.pallas_call(
        functools.partial(_pool_prompt_kernel, tm=tm),
        grid=(t // tm,),
        in_specs=[pl.BlockSpec((tm, D_MODEL), lambda i: (i, 0)),
                  const((1, 6 * D_MODEL)), const((1, D_MODEL)), const((1, D_MODEL)),
                  const(pool_w.shape), const((1, D_MODEL)),
                  const((N_EXPERTS, D_MODEL)), const((N_EXPERTS, 1))],
        out_specs=[pl.BlockSpec((tm, D_MODEL), lambda i: (i, 0)),
                   pl.BlockSpec((tm, D_MODEL), lambda i: (i, 0)),
                   pl.BlockSpec((N_EXPERTS, tm), lambda i: (0, i)),
                   const((HALO, D_MODEL))],
        out_shape=[jax.ShapeDtypeStruct((t, D_MODEL), F32),
                   jax.ShapeDtypeStruct((t, D_MODEL), BF16),
                   jax.ShapeDtypeStruct((N_EXPERTS, t), F32),
                   jax.ShapeDtypeStruct((HALO, D_MODEL), F32)],
        scratch_shapes=[pltpu.VMEM((tm + HALO, D_MODEL), F32)],
        compiler_params=_cparams("arbitrary"),
        name="pool_mixer_prompt",
    )(x, mod, gmix, gffn, pool_w, pool_scale, wrT, rb)


def _pool_sample_kernel(x_ref, st_ref, mod_ref, gmix_ref, gffn_ref, pw_ref, ps_ref, wrT_ref, rb_ref,
                        x1_ref, h2_ref, gT_ref, pool_ref, *, nb, ds):
    x = x_ref[...]
    mod6 = _split_mod(mod_ref[...])
    hs = _modulate(x, gmix_ref[...], mod6[0], mod6[1])
    slabs = [st_ref[r] for r in range(POOL_BUF)] + [hs[t * nb:(t + 1) * nb] for t in range(ds)]
    for r in range(POOL_BUF):
        pool_ref[r] = slabs[ds + r]
    ys = []
    for g, w in enumerate(POOL_WINDOWS):
        c0 = g * POOL_GROUP
        ds_rows = []
        for t in range(ds):
            p = POOL_BUF + t
            acc = slabs[p][:, c0:c0 + POOL_GROUP]
            for k in range(1, w):
                acc = acc + slabs[p - k][:, c0:c0 + POOL_GROUP]
            ds_rows.append(acc / float(w) - slabs[p][:, c0:c0 + POOL_GROUP])
        d = jnp.concatenate(ds_rows, axis=0)
        ys.append(jnp.dot(d.astype(BF16), pw_ref[g], preferred_element_type=F32))
    y = jnp.concatenate(ys, axis=1) * ps_ref[...]
    _ffn_prologue(x + mod6[2] * y, mod6, gffn_ref, wrT_ref, rb_ref, x1_ref, h2_ref, gT_ref)


def _pool_sample_call(x, state, mod, gmix, gffn, pool_w, pool_scale, wrT, rb, nb, ds):
    t = x.shape[0]
    full = lambda a: pl.BlockSpec(a.shape, lambda: (0,) * a.ndim)
    args = (x, state, mod, gmix, gffn, pool_w, pool_scale, wrT, rb)
    out_shape = [jax.ShapeDtypeStruct((t, D_MODEL), F32),
                 jax.ShapeDtypeStruct((t, D_MODEL), BF16),
                 jax.ShapeDtypeStruct((N_EXPERTS, t), F32),
                 jax.ShapeDtypeStruct((POOL_BUF, nb, D_MODEL), F32)]
    return pl.pallas_call(
        functools.partial(_pool_sample_kernel, nb=nb, ds=ds),
        in_specs=[full(a) for a in args],
        out_specs=[full(s) for s in out_shape],
        out_shape=out_shape,
        compiler_params=pltpu.CompilerParams(vmem_limit_bytes=VMEM_LIMIT),
        name="pool_mixer_sample",
    )(*args)


def _moe_kernel(h_ref, gates_ref, x_ref, g2_ref, wgu_ref, wd_ref, o_ref, acc_ref):
    e = pl.program_id(1)

    @pl.when(e == 0)
    def _():
        acc_ref[...] = jnp.zeros_like(acc_ref)

    gu = jnp.dot(h_ref[...], wgu_ref[0], preferred_element_type=F32)
    gate_in = gu[:, :D_EXPERT]
    a = gate_in * jax.nn.sigmoid(gate_in) * gu[:, D_EXPERT:]
    gates = gates_ref[...]
    lane = lax.broadcasted_iota(jnp.int32, gates.shape, 1)
    gcol = jnp.sum(jnp.where(lane == e, gates, 0.0), axis=1, keepdims=True)
    acc_ref[...] += jnp.dot((a * gcol).astype(BF16), wd_ref[0], preferred_element_type=F32)

    @pl.when(e == N_EXPERTS - 1)
    def _():
        o_ref[...] = x_ref[...] + g2_ref[...] * acc_ref[...]


def _moe_call(h, gates, x, g2, wgu, wd):
    t = h.shape[0]
    tm = min(MOE_ROW_TILE, t)
    g2_rows = g2.shape[0]
    g2_spec = (pl.BlockSpec((1, D_MODEL), lambda i, e: (0, 0)) if g2_rows == 1
               else pl.BlockSpec((tm, D_MODEL), lambda i, e: (i, 0)))
    return pl.pallas_call(
        _moe_kernel,
        grid=(t // tm, N_EXPERTS),
        in_specs=[pl.BlockSpec((tm, D_MODEL), lambda i, e: (i, 0)),
                  pl.BlockSpec((tm, N_EXPERTS), lambda i, e: (i, 0)),
                  pl.BlockSpec((tm, D_MODEL), lambda i, e: (i, 0)),
                  g2_spec,
                  pl.BlockSpec((1, D_MODEL, 2 * D_EXPERT), lambda i, e: (e, 0, 0)),
                  pl.BlockSpec((1, D_EXPERT, D_MODEL), lambda i, e: (e, 0, 0))],
        out_specs=pl.BlockSpec((tm, D_MODEL), lambda i, e: (i, 0)),
        out_shape=jax.ShapeDtypeStruct((t, D_MODEL), F32),
        scratch_shapes=[pltpu.VMEM((tm, D_MODEL), F32)],
        compiler_params=_cparams("arbitrary", "arbitrary"),
        name="moe_dense",
    )(h, gates, x, g2, wgu, wd)


def _seg_norm(t, g_full, seg_ref, segT_ref):
    sq = t * t
    hi = sq.astype(BF16)
    lo = (sq - hi.astype(F32)).astype(BF16)
    ssq = (jnp.dot(hi, seg_ref[...], preferred_element_type=F32)
           + jnp.dot(lo, seg_ref[...], preferred_element_type=F32))
    inv = lax.rsqrt(ssq * (1.0 / HEAD_QK) + RMS_EPS)
    ihi = inv.astype(BF16)
    ilo = (inv - ihi.astype(F32)).astype(BF16)
    inv_full = (jnp.dot(ihi, segT_ref[...], preferred_element_type=F32)
                + jnp.dot(ilo, segT_ref[...], preferred_element_type=F32))
    return t * inv_full * g_full


def _qkv_kernel(x_ref, mod_ref, gmix_ref, w_ref, gq_ref, gk_ref, seg_ref, segT_ref, *out_refs,
                prompt):
    mod6 = _split_mod(mod_ref[...])
    h = _modulate(x_ref[...], gmix_ref[...], mod6[0], mod6[1]).astype(BF16)
    qkv = jnp.dot(h, w_ref[...], preferred_element_type=F32)
    q = _seg_norm(qkv[:, :D_MODEL], gq_ref[...], seg_ref, segT_ref)
    k = _seg_norm(qkv[:, D_MODEL:2 * D_MODEL], gk_ref[...], seg_ref, segT_ref)
    v = qkv[:, 2 * D_MODEL:]
    q_ref, k32_ref, v32_ref = out_refs[:3]
    q_ref[...] = (q * (HEAD_QK ** -0.5 * LOG2E)).astype(BF16)
    k32_ref[...] = k
    v32_ref[...] = v
    if prompt:
        kb_ref, vT_ref = out_refs[3:]
        kb_ref[...] = k.astype(BF16)
        vT_ref[0] = v.T.astype(BF16)


def _qkv_call(x, mod, gmix, wqkv, gq, gk, seg, segT, prompt):
    t = x.shape[0]
    tm = min(ROW_TILE, t)
    const = lambda shape: pl.BlockSpec(shape, lambda i: (0,) * len(shape))
    row = lambda: pl.BlockSpec((tm, D_MODEL), lambda i: (i, 0))
    mod_spec = const((1, 6 * D_MODEL)) if mod.shape[0] == 1 else pl.BlockSpec((tm, 6 * D_MODEL), lambda i: (i, 0))
    out_specs = [row(), row(), row()]
    out_shape = [jax.ShapeDtypeStruct((t, D_MODEL), BF16),
                 jax.ShapeDtypeStruct((t, D_MODEL), F32),
                 jax.ShapeDtypeStruct((t, D_MODEL), F32)]
    if prompt:
        out_specs += [row(), pl.BlockSpec((1, D_MODEL, tm), lambda i: (i, 0, 0))]
        out_shape += [jax.ShapeDtypeStruct((t, D_MODEL), BF16),
                      jax.ShapeDtypeStruct((t // tm, D_MODEL, tm), BF16)]
    return pl.pallas_call(
        functools.partial(_qkv_kernel, prompt=prompt),
        grid=(t // tm,),
        in_specs=[row(), mod_spec, const((1, D_MODEL)), const(wqkv.shape),
                  const((1, D_MODEL)), const((1, D_MODEL)), const(seg.shape), const(segT.shape)],
        out_specs=out_specs,
        out_shape=out_shape,
        compiler_params=_cparams("arbitrary"),
        name="qkv_prompt" if prompt else "qkv_sample",
    )(x, mod, gmix, wqkv, gq, gk, seg, segT)


def _diff_lambda(lamv, lam0):
    a = jnp.sum(lamv[0:1] * lamv[1:2], axis=1, keepdims=True)
    b = jnp.sum(lamv[2:3] * lamv[3:4], axis=1, keepdims=True)
    return jnp.exp(a) - jnp.exp(b) + lam0


def _attn_prompt_kernel(slope_ref, q_ref, k_ref, vT_ref, bias_ref, lamv_ref, sg_ref, o_ref,
                        m_ref, l_ref, acc_ref, *, bq, bk, lam0):
    hd = pl.program_id(0)
    qi = pl.program_id(1)
    slope = slope_ref[hd]
    q0 = qi * bq

    qT = q_ref[...].astype(F32).T
    row = lax.broadcasted_iota(jnp.int32, qT.shape, 0)
    qp = jnp.concatenate([jnp.where(row < HEAD_QK, qT, 0.0), jnp.where(row >= HEAD_QK, qT, 0.0)],
                         axis=1).astype(BF16)
    m_ref[...] = jnp.full(m_ref.shape, NEG_BIG, F32)
    l_ref[...] = jnp.zeros_like(l_ref)
    acc_ref[...] = jnp.zeros_like(acc_ref)

    def step(j, masked):
        k0 = j * bk
        kblk = k_ref[pl.ds(pl.multiple_of(k0, bk), bk), :]
        s = jnp.dot(kblk, qp, preferred_element_type=F32) + bias_ref[0]
        if masked:
            kpos = k0 + lax.broadcasted_iota(jnp.int32, s.shape, 0)
            qcol = lax.broadcasted_iota(jnp.int32, s.shape, 1)
            qpos = q0 + jnp.where(qcol >= bq, qcol - bq, qcol)
            s = jnp.where(kpos <= qpos, s, -jnp.inf)
        shift = slope * (k0 - q0).astype(F32)
        m_old = m_ref[...]
        m_new = jnp.maximum(m_old, jnp.max(s, axis=0, keepdims=True) + shift)
        alpha = jnp.exp2(m_old - m_new)
        p = jnp.exp2(s - (m_new - shift))
        l_ref[...] = alpha * l_ref[...] + jnp.sum(p, axis=0, keepdims=True)
        acc_ref[...] = alpha * acc_ref[...] + jnp.dot(vT_ref[j], p.astype(BF16),
                                                      preferred_element_type=F32)
        m_ref[...] = m_new

    jd = q0 // bk

    def body(j, carry):
        step(j, False)
        return carry

    lax.fori_loop(0, jd, body, 0)
    step(jd, True)

    o_all = acc_ref[...] / l_ref[...]
    lam = _diff_lambda(lamv_ref[...], lam0)
    o = o_all[:, :bq] - lam * o_all[:, bq:]
    ms = jnp.mean(o * o, axis=0, keepdims=True)
    o = o * lax.rsqrt(ms + RMS_EPS) * sg_ref[...] * (1.0 - lam0)
    o_ref[...] = o.T.astype(BF16)


def _attn_prompt_call(slopes_l2, q, kb, vT, bias, lamv, sg, lam0):
    t = q.shape[0]
    bq, bk = ATTN_BQ, ATTN_BK
    nk = t // bk
    return pl.pallas_call(
        functools.partial(_attn_prompt_kernel, bq=bq, bk=bk, lam0=lam0),
        grid=(N_HEADS, t // bq),
        in_specs=[pl.BlockSpec(memory_space=pltpu.SMEM),
                  pl.BlockSpec((bq, KEY_DIM), lambda h, i: (i, h)),
                  pl.BlockSpec((t, KEY_DIM), lambda h, i: (0, h)),
                  pl.BlockSpec((nk, HEAD_V, bk), lambda h, i: (0, h, 0)),
                  pl.BlockSpec((1, bk, 2 * bq), lambda h, i: (h, 0, 0)),
                  pl.BlockSpec(lamv.shape, lambda h, i: (0, 0)),
                  pl.BlockSpec(sg.shape, lambda h, i: (0, 0))],
        out_specs=pl.BlockSpec((bq, HEAD_V), lambda h, i: (i, h)),
        out_shape=jax.ShapeDtypeStruct((t, D_MODEL), BF16),
        scratch_shapes=[pltpu.VMEM((1, 2 * bq), F32), pltpu.VMEM((1, 2 * bq), F32),
                        pltpu.VMEM((HEAD_V, 2 * bq), F32)],
        compiler_params=_cparams("arbitrary", "arbitrary"),
        name="diff_attn_prompt",
    )(slopes_l2, q, kb, vT, bias, lamv, sg)


def _attn_sample_kernel(pt_ref, *refs, n_pages, page, ds, lam0):
    k_refs = refs[:n_pages]
    v_refs = refs[n_pages:2 * n_pages]
    (qbd_ref, bias_ref, slope_ref, knew_ref, vnew_ref, bnew_ref, lamv_ref, sg_ref,
     o_ref, kc_ref, vc_ref, m_ref, l_ref, acc_ref) = refs[2 * n_pages:]
    c = pl.program_id(1)
    nc = pl.num_programs(1)
    chunk = n_pages * page

    @pl.when(c == 0)
    def _():
        m_ref[...] = jnp.full(m_ref.shape, NEG_BIG, F32)
        l_ref[...] = jnp.zeros_like(l_ref)
        acc_ref[...] = jnp.zeros_like(acc_ref)

    def update(s, shift, v_bf16):
        m_old = m_ref[...]
        m_new = jnp.maximum(m_old, jnp.max(s, axis=1, keepdims=True) + shift)
        alpha = jnp.exp2(m_old - m_new)
        p = jnp.exp2(s - (m_new - shift))
        l_ref[...] = alpha * l_ref[...] + jnp.sum(p, axis=1, keepdims=True)
        acc_ref[...] = alpha * acc_ref[...] + jnp.dot(p.astype(BF16), v_bf16,
                                                      preferred_element_type=F32)
        m_ref[...] = m_new

    for i in range(n_pages):
        kc_ref[i * page:(i + 1) * page, :] = k_refs[i][0].astype(BF16)
        vc_ref[i * page:(i + 1) * page, :] = v_refs[i][0].astype(BF16)
    sT = jnp.dot(kc_ref[...], qbd_ref[0], preferred_element_type=F32)
    s = sT.T + bias_ref[...]
    past = nc * chunk
    shift = slope_ref[...] * (c * chunk - past).astype(F32)
    update(s, shift, vc_ref[...])

    @pl.when(c == nc - 1)
    def _():
        sTn = jnp.dot(knew_ref[0], qbd_ref[0], preferred_element_type=F32)
        sn = sTn.T + bnew_ref[...]
        update(sn, jnp.zeros_like(slope_ref[...]), vnew_ref[0])

        lam = _diff_lambda(lamv_ref[...], lam0)
        acc = acc_ref[...]
        l = l_ref[...]
        rows = 2 * ds
        for hd in range(N_HEADS):
            blk = acc[hd * rows:(hd + 1) * rows, hd * HEAD_V:(hd + 1) * HEAD_V] / l[hd * rows:(hd + 1) * rows]
            o = blk[0:ds] - lam * blk[ds:rows]
            ms = jnp.mean(o * o, axis=1, keepdims=True)
            o = o * lax.rsqrt(ms + RMS_EPS) * sg_ref[...] * (1.0 - lam0)
            o_ref[0, 0:ds, hd * HEAD_V:(hd + 1) * HEAD_V] = o
            o_ref[0, ds:rows, hd * HEAD_V:(hd + 1) * HEAD_V] = jnp.zeros((rows - ds, HEAD_V), F32)


def _attn_sample_call(page_table, cache_k, cache_v, qbd, bias, slope_col, knew, vnew, bnew, lamv, sg_row,
                      lam0, ds):
    nb, n_pages_total = page_table.shape
    page = cache_k.shape[1]
    n_pages = PAGES_PER_STEP
    chunk = n_pages * page
    rows = 2 * ds
    page_spec = lambda i: pl.BlockSpec((1, page, D_MODEL),
                                       lambda b, c, pt, i=i: (pt[b, c * n_pages + i], 0, 0))
    const2 = lambda a: pl.BlockSpec(a.shape, lambda b, c, pt: (0, 0))
    per_seq = lambda a: pl.BlockSpec((1,) + a.shape[1:], lambda b, c, pt: (b, 0, 0))
    grid_spec = pltpu.PrefetchScalarGridSpec(
        num_scalar_prefetch=1,
        grid=(nb, n_pages_total // n_pages),
        in_specs=([page_spec(i) for i in range(n_pages)] + [page_spec(i) for i in range(n_pages)]
                  + [per_seq(qbd), const2(bias), const2(slope_col), per_seq(knew), per_seq(vnew),
                     const2(bnew), const2(lamv), const2(sg_row)]),
        out_specs=pl.BlockSpec((1, rows, D_MODEL), lambda b, c, pt: (b, 0, 0)),
        scratch_shapes=[pltpu.VMEM((chunk, D_MODEL), BF16), pltpu.VMEM((chunk, D_MODEL), BF16),
                        pltpu.VMEM((LANES, 1), F32), pltpu.VMEM((LANES, 1), F32),
                        pltpu.VMEM((LANES, D_MODEL), F32)])
    return pl.pallas_call(
        functools.partial(_attn_sample_kernel, n_pages=n_pages, page=page, ds=ds, lam0=lam0),
        grid_spec=grid_spec,
        out_shape=jax.ShapeDtypeStruct((nb, rows, D_MODEL), F32),
        compiler_params=_cparams("arbitrary", "arbitrary"),
        name="diff_attn_sample",
    )(page_table, *([cache_k] * n_pages), *([cache_v] * n_pages), qbd, bias, slope_col, knew, vnew,
      bnew, lamv, sg_row)


def _oproj_kernel(o_ref, x_ref, mod_ref, wo_ref, gffn_ref, wrT_ref, rb_ref, x1_ref, h2_ref, gT_ref):
    mod6 = _split_mod(mod_ref[...])
    y = jnp.dot(o_ref[...], wo_ref[...], preferred_element_type=F32)
    _ffn_prologue(x_ref[...] + mod6[2] * y, mod6, gffn_ref, wrT_ref, rb_ref, x1_ref, h2_ref, gT_ref)


def _oproj_call(o, x, mod, wo, gffn, wrT, rb):
    t = x.shape[0]
    tm = min(ROW_TILE, t)
    const = lambda shape: pl.BlockSpec(shape, lambda i: (0,) * len(shape))
    row = lambda: pl.BlockSpec((tm, D_MODEL), lambda i: (i, 0))
    mod_spec = const((1, 6 * D_MODEL)) if mod.shape[0] == 1 else pl.BlockSpec((tm, 6 * D_MODEL), lambda i: (i, 0))
    return pl.pallas_call(
        _oproj_kernel,
        grid=(t // tm,),
        in_specs=[row(), row(), mod_spec, const(wo.shape), const((1, D_MODEL)),
                  const((N_EXPERTS, D_MODEL)), const((N_EXPERTS, 1))],
        out_specs=[row(), row(), pl.BlockSpec((N_EXPERTS, tm), lambda i: (0, i))],
        out_shape=[jax.ShapeDtypeStruct((t, D_MODEL), F32),
                   jax.ShapeDtypeStruct((t, D_MODEL), BF16),
                   jax.ShapeDtypeStruct((N_EXPERTS, t), F32)],
        compiler_params=_cparams("arbitrary"),
        name="attn_out_proj",
    )(o, x, mod, wo, gffn, wrT, rb)


def kernel(x_prompt, x_sample, state_pool, cache_k, cache_v, page_table, c_prompt, c_sample,
           w_mod, b_mod, norm_mix_g, norm_ffn_g, pool_w, pool_scale, w_qkv, q_norm_g, k_norm_g,
           lambda_q1, lambda_k1, lambda_q2, lambda_k2, subln_g, w_o, w_router, router_bias,
           w_gate, w_up, w_down):
    assert DEPTH == 2 and x_prompt.shape[0] == 1
    seq = x_prompt.shape[1]
    nb, ds, _ = x_sample.shape
    n_phys, page = cache_k.shape[1], cache_k.shape[2]
    past = page_table.shape[1] * page
    ts = nb * ds

    c_all = jnp.concatenate([c_prompt, jnp.zeros((7, D_MODEL), F32), c_sample], axis=0)
    mods = _mod_call(c_all, w_mod, b_mod)
    mod_p = mods[:, 0:1]
    mod_s = jnp.tile(mods[:, 8:8 + nb], (1, ds, 1))

    pool_w_b = pool_w.astype(BF16)
    wgu = jnp.concatenate([w_gate, w_up], axis=-1).astype(BF16)
    wd = w_down.astype(BF16)
    wqkv = w_qkv.astype(BF16)
    wo = w_o.astype(BF16)
    wrT = w_router.T
    rb = router_bias.reshape(N_EXPERTS, 1)
    gq = jnp.tile(q_norm_g.reshape(1, 1, KEY_DIM), (1, N_HEADS, 1)).reshape(-1, 1, D_MODEL)
    gk = jnp.tile(k_norm_g.reshape(1, 1, KEY_DIM), (1, N_HEADS, 1)).reshape(-1, 1, D_MODEL)
    seg_id = jnp.arange(D_MODEL) // HEAD_QK
    seg = (seg_id[:, None] == jnp.arange(LANES)[None, :]).astype(BF16)
    segT = seg.T
    slopes = 2.0 ** (-8.0 * jnp.arange(1, N_HEADS + 1, dtype=F32) / N_HEADS)
    slopes_l2 = slopes * LOG2E

    xs = x_sample.transpose(1, 0, 2).reshape(ts, D_MODEL)
    state = state_pool.transpose(0, 2, 1, 3)

    x1p, h2p, gTp, pst = _pool_prompt_call(x_prompt[0], mod_p[0], norm_mix_g[0:1], norm_ffn_g[0:1],
                                           pool_w_b[0], pool_scale[0:1], wrT, rb)
    x1s, h2s, gTs, pool_s = _pool_sample_call(xs, state[0], mod_s[0], norm_mix_g[0:1], norm_ffn_g[0:1],
                                              pool_w_b[0], pool_scale[0:1], wrT, rb, nb, ds)
    xp = _moe_call(h2p, gTp.T, x1p, mod_p[0][:, 5 * D_MODEL:], wgu[0], wd[0])
    xs = _moe_call(h2s, gTs.T, x1s, mod_s[0][:, 5 * D_MODEL:], wgu[0], wd[0])

    lam0 = _lambda_init(1)
    lamv = jnp.stack([lambda_q1[0], lambda_k1[0], lambda_q2[0], lambda_k2[0]])
    qb, k32, v32, kb, vT = _qkv_call(xp, mod_p[1], norm_mix_g[1:2], wqkv[0], gq[0], gk[0], seg, segT, True)
    kloc = jnp.arange(ATTN_BK, dtype=F32)
    bias_p = jnp.broadcast_to((slopes_l2[:, None] * kloc[None, :])[:, :, None],
                              (N_HEADS, ATTN_BK, 2 * ATTN_BQ))
    o_p = _attn_prompt_call(slopes_l2, qb, kb, vT, bias_p, lamv, subln_g[0].reshape(HEAD_V, 1), lam0)
    x1p, h2p, gTp = _oproj_call(o_p, xp, mod_p[1], wo[0], norm_ffn_g[1:2], wrT, rb)
    yp = _moe_call(h2p, gTp.T, x1p, mod_p[1][:, 5 * D_MODEL:], wgu[1], wd[1])

    qs, ks32, vs32 = _qkv_call(xs, mod_s[1], norm_mix_g[1:2], wqkv[0], gq[0], gk[0], seg, segT, False)
    rows = 2 * ds
    q_seq = qs.astype(F32).reshape(ds, nb, D_MODEL).transpose(1, 2, 0)
    colmask = (jnp.arange(D_MODEL)[:, None] // HEAD_QK) == (jnp.arange(N_HEADS * rows)[None, :] // ds)
    qbd = jnp.where(colmask[None], jnp.tile(q_seq, (1, 1, 2 * N_HEADS)), 0.0)
    qbd = jnp.pad(qbd, ((0, 0), (0, 0), (0, LANES - N_HEADS * rows))).astype(BF16)
    row_slope = jnp.pad(jnp.repeat(slopes_l2, rows), (0, LANES - N_HEADS * rows))
    chunk = PAGES_PER_STEP * page
    bias_s = row_slope[:, None] * jnp.arange(chunk, dtype=F32)[None, :]
    jn = jnp.arange(LANES)[None, :]
    tok = (jnp.arange(LANES) % ds)[:, None]
    bias_new = jnp.where(jnp.logical_and(jn <= tok, jn < ds), row_slope[:, None] * jn.astype(F32), -jnp.inf)
    to_seq = lambda a: jnp.pad(a.reshape(ds, nb, D_MODEL).transpose(1, 0, 2),
                               ((0, 0), (0, LANES - ds), (0, 0))).astype(BF16)
    o_s = _attn_sample_call(page_table, cache_k[0].reshape(n_phys, page, D_MODEL),
                            cache_v[0].reshape(n_phys, page, D_MODEL), qbd, bias_s,
                            row_slope.reshape(LANES, 1), to_seq(ks32), to_seq(vs32), bias_new, lamv,
                            subln_g[0].reshape(1, HEAD_V), lam0, ds)
    o_s = o_s[:, :ds].transpose(1, 0, 2).reshape(ts, D_MODEL).astype(BF16)
    x1s, h2s, gTs = _oproj_call(o_s, xs, mod_s[1], wo[0], norm_ffn_g[1:2], wrT, rb)
    ys = _moe_call(h2s, gTs.T, x1s, mod_s[1][:, 5 * D_MODEL:], wgu[1], wd[1])

    seq_major = lambda a: a.reshape(ds, nb, D_MODEL).transpose(1, 0, 2)
    return (yp[None],
            seq_major(ys),
            pst[None, None, 1:],
            pool_s.transpose(1, 0, 2)[None],
            k32.reshape(1, 1, seq // page, page, N_HEADS, KEY_DIM),
            v32.reshape(1, 1, seq // page, page, N_HEADS, HEAD_V),
            seq_major(ks32).reshape(1, nb, ds, N_HEADS, KEY_DIM),
            seq_major(vs32).reshape(1, nb, ds, N_HEADS, HEAD_V))
```

```python
import functools
import math

import jax
import jax.numpy as jnp
from jax import lax
from jax.experimental import pallas as pl
from jax.experimental.pallas import tpu as pltpu

F32 = jnp.float32
BF16 = jnp.bfloat16
HIGHEST = lax.Precision.HIGHEST

D_MODEL = 1024
DEPTH = 2
POOL_WINDOWS = (2, 4, 8, 16)
POOL_GROUP = D_MODEL // len(POOL_WINDOWS)
POOL_BUF = max(POOL_WINDOWS) - 1
HALO = POOL_BUF + 1
N_HEADS = 8
HEAD_QK = 64
HEAD_V = 2 * HEAD_QK
KEY_DIM = 2 * HEAD_QK
N_EXPERTS = 16
EXPERTS_PER_GROUP = 4
N_GROUPS = N_EXPERTS // EXPERTS_PER_GROUP
D_EXPERT = D_MODEL // 2
RMS_EPS = 1e-6
LOG2E = 1.4426950408889634
LANES = 128
NEG_BIG = -1e30

ROW_TILE = 512
MOE_ROW_TILE = 1024
ATTN_BQ = ROW_TILE
ATTN_BK = ROW_TILE
ALIBI_PARTS = 3
DENOM_ROWS = 16
PAGES_PER_STEP = 8
VMEM_LIMIT = 56 * 1024 * 1024


def _cparams(*sem):
    return pltpu.CompilerParams(dimension_semantics=sem, vmem_limit_bytes=VMEM_LIMIT)


def _lambda_init(layer):
    return 0.8 - 0.6 * math.exp(-0.3 * layer)


def _truncate_to_bf16(x):
    bits = lax.bitcast_convert_type(x, jnp.uint32) & jnp.uint32(0xFFFF0000)
    return lax.bitcast_convert_type(bits, F32)


def _rms(x, g):
    ms = jnp.mean(x * x, axis=-1, keepdims=True)
    return x * lax.rsqrt(ms + RMS_EPS) * g


def _modulate(x, g, shift, scale):
    return _rms(x, g) * (1.0 + scale) + shift


def _split_mod(mod):
    return [mod[:, k * D_MODEL:(k + 1) * D_MODEL] for k in range(6)]


def _route_gates(h, wrT_ref, rb_ref, gT_ref):
    logits = lax.dot_general(wrT_ref[...], h, (((1,), (1,)), ((), ())),
                             precision=HIGHEST, preferred_element_type=F32)
    mx = jnp.max(logits, axis=0, keepdims=True)
    ex = jnp.exp(logits - mx)
    probs = ex / jnp.sum(ex, axis=0, keepdims=True)
    sel = probs + rb_ref[...]
    srow = [sel[e:e + 1] for e in range(N_EXPERTS)]
    prow = [probs[e:e + 1] for e in range(N_EXPERTS)]

    gscore = []
    for g in range(N_GROUPS):
        r = srow[g * EXPERTS_PER_GROUP:(g + 1) * EXPERTS_PER_GROUP]
        best = None
        for a in range(EXPERTS_PER_GROUP):
            for b in range(a + 1, EXPERTS_PER_GROUP):
                s = r[a] + r[b]
                best = s if best is None else jnp.maximum(best, s)
        gscore.append(best)
    gbest = functools.reduce(jnp.maximum, gscore)
    in_grp, taken = [], None
    for g in range(N_GROUPS):
        hit = gscore[g] == gbest
        if taken is not None:
            hit = jnp.logical_and(hit, jnp.logical_not(taken))
        taken = hit if taken is None else jnp.logical_or(taken, hit)
        in_grp.append(hit)

    def pick(rows, k):
        out = rows[(N_GROUPS - 1) * EXPERTS_PER_GROUP + k]
        for g in range(N_GROUPS - 2, -1, -1):
            out = jnp.where(in_grp[g], rows[g * EXPERTS_PER_GROUP + k], out)
        return out

    v = [pick(srow, k) for k in range(EXPERTS_PER_GROUP)]
    p = [pick(prow, k) for k in range(EXPERTS_PER_GROUP)]

    def first_hits(vals, target):
        hits, seen = [], None
        for x in vals:
            hit = x == target
            if seen is not None:
                hit = jnp.logical_and(hit, jnp.logical_not(seen))
            seen = hit if seen is None else jnp.logical_or(seen, hit)
            hits.append(hit)
        return hits

    top1 = first_hits(v, functools.reduce(jnp.maximum, v))
    rest = [jnp.where(top1[k], -jnp.inf, v[k]) for k in range(EXPERTS_PER_GROUP)]
    top2 = first_hits(rest, functools.reduce(jnp.maximum, rest))
    p1 = functools.reduce(jnp.add, [jnp.where(top1[k], p[k], 0.0) for k in range(EXPERTS_PER_GROUP)])
    p2 = functools.reduce(jnp.add, [jnp.where(top2[k], p[k], 0.0) for k in range(EXPERTS_PER_GROUP)])
    den = p1 + p2
    w1, w2 = p1 / den, p2 / den
    for g in range(N_GROUPS):
        for k in range(EXPERTS_PER_GROUP):
            val = jnp.where(top1[k], w1, jnp.where(top2[k], w2, 0.0))
            e = g * EXPERTS_PER_GROUP + k
            gT_ref[e:e + 1, :] = jnp.where(in_grp[g], val, 0.0)


def _ffn_prologue(x1, mod6, gffn_ref, wrT_ref, rb_ref, x1_ref, h2_ref, gT_ref):
    x1_ref[...] = x1
    h2 = _modulate(x1, gffn_ref[...], mod6[3], mod6[4])
    h2_ref[...] = h2.astype(BF16)
    _route_gates(h2, wrT_ref, rb_ref, gT_ref)


def _mod_kernel(c_ref, w_ref, b_ref, o_ref):
    c = c_ref[...]
    a = c * jax.nn.sigmoid(c)
    o_ref[0] = jnp.dot(a, w_ref[0], precision=HIGHEST, preferred_element_type=F32) + b_ref[0]


def _mod_call(c_all, w_mod, b_mod):
    rows = c_all.shape[0]
    tn = 1536
    return pl.pallas_call(
        _mod_kernel,
        grid=(DEPTH, 6 * D_MODEL // tn),
        in_specs=[pl.BlockSpec((rows, D_MODEL), lambda i, j: (0, 0)),
                  pl.BlockSpec((1, D_MODEL, tn), lambda i, j: (i, 0, j)),
                  pl.BlockSpec((1, 1, tn), lambda i, j: (i, 0, j))],
        out_specs=pl.BlockSpec((1, rows, tn), lambda i, j: (i, 0, j)),
        out_shape=jax.ShapeDtypeStruct((DEPTH, rows, 6 * D_MODEL), F32),
        compiler_params=_cparams("arbitrary", "arbitrary"),
        name="adaln_mod",
    )(c_all, w_mod, b_mod.reshape(DEPTH, 1, 6 * D_MODEL))


def _pool_prompt_kernel(x_ref, mod_ref, gmix_ref, gffn_ref, pw_ref, ps_ref, wrT_ref, rb_ref,
                        x1_ref, h2_ref, gT_ref, pst_ref, ext_ref, *, tm):
    i = pl.program_id(0)

    @pl.when(i == 0)
    def _():
        ext_ref[0:HALO, :] = jnp.zeros((HALO, D_MODEL), F32)

    x = x_ref[...]
    mod6 = _split_mod(mod_ref[...])
    ext_ref[HALO:HALO + tm, :] = _modulate(x, gmix_ref[...], mod6[0], mod6[1])
    pos = i * tm + lax.broadcasted_iota(jnp.int32, (tm, 1), 0)
    ys = []
    for g, w in enumerate(POOL_WINDOWS):
        c0 = g * POOL_GROUP
        cur = ext_ref[HALO:HALO + tm, c0:c0 + POOL_GROUP]
        acc = cur
        for k in range(1, w):
            acc = acc + ext_ref[HALO - k:HALO - k + tm, c0:c0 + POOL_GROUP]
        cnt = jnp.minimum(pos + 1, w).astype(F32)
        d = acc / cnt - cur
        ys.append(jnp.dot(d.astype(BF16), pw_ref[g], preferred_element_type=F32))
    y = jnp.concatenate(ys, axis=1) * ps_ref[...]
    tail = ext_ref[tm:tm + HALO, :]
    ext_ref[0:HALO, :] = tail
    pst_ref[...] = tail
    _ffn_prologue(x + mod6[2] * y, mod6, gffn_ref, wrT_ref, rb_ref, x1_ref, h2_ref, gT_ref)


def _pool_prompt_call(x, mod, gmix, gffn, pool_w, pool_scale, wrT, rb):
    t = x.shape[0]
    tm = ROW_TILE
    const = lambda shape: pl.BlockSpec(shape, lambda i: (0,) * len(shape))
    return pl.pallas_call(
        functools.partial(_pool_prompt_kernel, tm=tm),
        grid=(t // tm,),
        in_specs=[pl.BlockSpec((tm, D_MODEL), lambda i: (i, 0)),
                  const((1, 6 * D_MODEL)), const((1, D_MODEL)), const((1, D_MODEL)),
                  const(pool_w.shape), const((1, D_MODEL)),
                  const((N_EXPERTS, D_MODEL)), const((N_EXPERTS, 1))],
        out_specs=[pl.BlockSpec((tm, D_MODEL), lambda i: (i, 0)),
                   pl.BlockSpec((tm, D_MODEL), lambda i: (i, 0)),
                   pl.BlockSpec((N_EXPERTS, tm), lambda i: (0, i)),
                   const((HALO, D_MODEL))],
        out_shape=[jax.ShapeDtypeStruct((t, D_MODEL), F32),
                   jax.ShapeDtypeStruct((t, D_MODEL), BF16),
                   jax.ShapeDtypeStruct((N_EXPERTS, t), F32),
                   jax.ShapeDtypeStruct((HALO, D_MODEL), F32)],
        scratch_shapes=[pltpu.VMEM((tm + HALO, D_MODEL), F32)],
        compiler_params=_cparams("arbitrary"),
        name="pool_mixer_prompt",
    )(x, mod, gmix, gffn, pool_w, pool_scale, wrT, rb)


def _pool_sample_kernel(x_ref, st_ref, mod_ref, gmix_ref, gffn_ref, pw_ref, ps_ref, wrT_ref, rb_ref,
                        x1_ref, h2_ref, gT_ref, pool_ref, *, nb, ds):
    x = x_ref[...]
    mod6 = _split_mod(mod_ref[...])
    hs = _modulate(x, gmix_ref[...], mod6[0], mod6[1])
    slabs = [st_ref[r] for r in range(POOL_BUF)] + [hs[t * nb:(t + 1) * nb] for t in range(ds)]
    for r in range(POOL_BUF):
        pool_ref[r] = slabs[ds + r]
    ys = []
    for g, w in enumerate(POOL_WINDOWS):
        c0 = g * POOL_GROUP
        ds_rows = []
        for t in range(ds):
            p = POOL_BUF + t
            acc = slabs[p][:, c0:c0 + POOL_GROUP]
            for k in range(1, w):
                acc = acc + slabs[p - k][:, c0:c0 + POOL_GROUP]
            ds_rows.append(acc / float(w) - slabs[p][:, c0:c0 + POOL_GROUP])
        d = jnp.concatenate(ds_rows, axis=0)
        ys.append(jnp.dot(d.astype(BF16), pw_ref[g], preferred_element_type=F32))
    y = jnp.concatenate(ys, axis=1) * ps_ref[...]
    _ffn_prologue(x + mod6[2] * y, mod6, gffn_ref, wrT_ref, rb_ref, x1_ref, h2_ref, gT_ref)


def _pool_sample_call(x, state, mod, gmix, gffn, pool_w, pool_scale, wrT, rb, nb, ds):
    t = x.shape[0]
    full = lambda a: pl.BlockSpec(a.shape, lambda: (0,) * a.ndim)
    args = (x, state, mod, gmix, gffn, pool_w, pool_scale, wrT, rb)
    out_shape = [jax.ShapeDtypeStruct((t, D_MODEL), F32),
                 jax.ShapeDtypeStruct((t, D_MODEL), BF16),
                 jax.ShapeDtypeStruct((N_EXPERTS, t), F32),
                 jax.ShapeDtypeStruct((POOL_BUF, nb, D_MODEL), F32)]
    return pl.pallas_call(
        functools.partial(_pool_sample_kernel, nb=nb, ds=ds),
        in_specs=[full(a) for a in args],
        out_specs=[full(s) for s in out_shape],
        out_shape=out_shape,
        compiler_params=pltpu.CompilerParams(vmem_limit_bytes=VMEM_LIMIT),
        name="pool_mixer_sample",
    )(*args)


def _moe_kernel(h_ref, gates_ref, x_ref, g2_ref, wgu_ref, wd_ref, o_ref, acc_ref):
    e = pl.program_id(1)

    @pl.when(e == 0)
    def _():
        acc_ref[...] = jnp.zeros_like(acc_ref)

    gu = jnp.dot(h_ref[...], wgu_ref[0], preferred_element_type=F32)
    gate_in = gu[:, :D_EXPERT]
    a = gate_in * jax.nn.sigmoid(gate_in) * gu[:, D_EXPERT:]
    gates = gates_ref[...]
    lane = lax.broadcasted_iota(jnp.int32, gates.shape, 1)
    gcol = jnp.sum(jnp.where(lane == e, gates, 0.0), axis=1, keepdims=True)
    acc_ref[...] += jnp.dot((a * gcol).astype(BF16), wd_ref[0], preferred_element_type=F32)

    @pl.when(e == N_EXPERTS - 1)
    def _():
        o_ref[...] = x_ref[...] + g2_ref[...] * acc_ref[...]


def _moe_call(h, gates, x, g2, wgu, wd):
    t = h.shape[0]
    tm = min(MOE_ROW_TILE, t)
    g2_rows = g2.shape[0]
    g2_spec = (pl.BlockSpec((1, D_MODEL), lambda i, e: (0, 0)) if g2_rows == 1
               else pl.BlockSpec((tm, D_MODEL), lambda i, e: (i, 0)))
    return pl.pallas_call(
        _moe_kernel,
        grid=(t // tm, N_EXPERTS),
        in_specs=[pl.BlockSpec((tm, D_MODEL), lambda i, e: (i, 0)),
                  pl.BlockSpec((tm, N_EXPERTS), lambda i, e: (i, 0)),
                  pl.BlockSpec((tm, D_MODEL), lambda i, e: (i, 0)),
                  g2_spec,
                  pl.BlockSpec((1, D_MODEL, 2 * D_EXPERT), lambda i, e: (e, 0, 0)),
                  pl.BlockSpec((1, D_EXPERT, D_MODEL), lambda i, e: (e, 0, 0))],
        out_specs=pl.BlockSpec((tm, D_MODEL), lambda i, e: (i, 0)),
        out_shape=jax.ShapeDtypeStruct((t, D_MODEL), F32),
        scratch_shapes=[pltpu.VMEM((tm, D_MODEL), F32)],
        compiler_params=_cparams("arbitrary", "arbitrary"),
        name="moe_dense",
    )(h, gates, x, g2, wgu, wd)


def _seg_norm(t, g_full, seg_ref, segT_ref):
    sq = t * t
    hi = sq.astype(BF16)
    lo = (sq - hi.astype(F32)).astype(BF16)
    ssq = (jnp.dot(hi, seg_ref[...], preferred_element_type=F32)
           + jnp.dot(lo, seg_ref[...], preferred_element_type=F32))
    inv = lax.rsqrt(ssq * (1.0 / HEAD_QK) + RMS_EPS)
    ihi = inv.astype(BF16)
    ilo = (inv - ihi.astype(F32)).astype(BF16)
    inv_full = (jnp.dot(ihi, segT_ref[...], preferred_element_type=F32)
                + jnp.dot(ilo, segT_ref[...], preferred_element_type=F32))
    return t * inv_full * g_full


def _qkv_kernel(x_ref, mod_ref, gmix_ref, w_ref, gq_ref, gk_ref, seg_ref, segT_ref, *out_refs,
                prompt):
    mod6 = _split_mod(mod_ref[...])
    h = _modulate(x_ref[...], gmix_ref[...], mod6[0], mod6[1]).astype(BF16)
    qkv = jnp.dot(h, w_ref[...], preferred_element_type=F32)
    q = _seg_norm(qkv[:, :D_MODEL], gq_ref[...], seg_ref, segT_ref)
    k = _seg_norm(qkv[:, D_MODEL:2 * D_MODEL], gk_ref[...], seg_ref, segT_ref)
    v = qkv[:, 2 * D_MODEL:]
    q_ref, k32_ref, v32_ref = out_refs[:3]
    q_ref[...] = (q * (HEAD_QK ** -0.5 * LOG2E)).astype(BF16)
    k32_ref[...] = k
    v32_ref[...] = v
    if prompt:
        kb_ref, vT_ref = out_refs[3:]
        kb_ref[...] = k.astype(BF16)
        vT_ref[0] = v.T.astype(BF16)


def _qkv_call(x, mod, gmix, wqkv, gq, gk, seg, segT, prompt):
    t = x.shape[0]
    tm = min(ROW_TILE, t)
    const = lambda shape: pl.BlockSpec(shape, lambda i: (0,) * len(shape))
    row = lambda: pl.BlockSpec((tm, D_MODEL), lambda i: (i, 0))
    mod_spec = const((1, 6 * D_MODEL)) if mod.shape[0] == 1 else pl.BlockSpec((tm, 6 * D_MODEL), lambda i: (i, 0))
    out_specs = [row(), row(), row()]
    out_shape = [jax.ShapeDtypeStruct((t, D_MODEL), BF16),
                 jax.ShapeDtypeStruct((t, D_MODEL), F32),
                 jax.ShapeDtypeStruct((t, D_MODEL), F32)]
    if prompt:
        out_specs += [row(), pl.BlockSpec((1, D_MODEL, tm), lambda i: (i, 0, 0))]
        out_shape += [jax.ShapeDtypeStruct((t, D_MODEL), BF16),
                      jax.ShapeDtypeStruct((t // tm, D_MODEL, tm), BF16)]
    return pl.pallas_call(
        functools.partial(_qkv_kernel, prompt=prompt),
        grid=(t // tm,),
        in_specs=[row(), mod_spec, const((1, D_MODEL)), const(wqkv.shape),
                  const((1, D_MODEL)), const((1, D_MODEL)), const(seg.shape), const(segT.shape)],
        out_specs=out_specs,
        out_shape=out_shape,
        compiler_params=_cparams("arbitrary"),
        name="qkv_prompt" if prompt else "qkv_sample",
    )(x, mod, gmix, wqkv, gq, gk, seg, segT)


def _diff_lambda(lamv, lam0):
    a = jnp.sum(lamv[0:1] * lamv[1:2], axis=1, keepdims=True)
    b = jnp.sum(lamv[2:3] * lamv[3:4], axis=1, keepdims=True)
    return jnp.exp(a) - jnp.exp(b) + lam0


def _attn_prompt_kernel(slope_ref, q_ref, k_ref, vT_ref, kbias_ref, lamv_ref, sg_ref, o_ref,
                        sa_ref, sb_ref, p_ref, m_ref, alpha_ref, acc_ref, *, bq, bk, lam0):
    hd = pl.program_id(0)
    qi = pl.program_id(1)
    slope = slope_ref[hd]
    q0 = qi * bq

    qT = q_ref[...].astype(F32).T
    row = lax.broadcasted_iota(jnp.int32, qT.shape, 0)
    row2 = lax.broadcasted_iota(jnp.int32, (KEY_DIM, 2 * bq), 0)
    qp = jnp.concatenate(
        [jnp.concatenate([jnp.where(row < HEAD_QK, qT, 0.0), jnp.where(row >= HEAD_QK, qT, 0.0)], axis=1),
         jnp.where(row2 < ALIBI_PARTS, 1.0, 0.0)], axis=0).astype(BF16)
    kbias = kbias_ref[0]
    ones_rows = jnp.ones((DENOM_ROWS, bk), BF16)

    def scores(j):
        kblk = k_ref[pl.ds(pl.multiple_of(j * bk, bk), bk), :]
        return jnp.dot(jnp.concatenate([kblk, kbias], axis=1), qp,
                       preferred_element_type=F32)

    def softmax(s_ref, j, masked):
        s = s_ref[...]
        if masked:
            kk = lax.broadcasted_iota(jnp.int32, s.shape, 0)
            qq = lax.broadcasted_iota(jnp.int32, s.shape, 1)
            s = jnp.where(kk <= jnp.where(qq >= bq, qq - bq, qq), s, -jnp.inf)
        shift = slope * (j * bk - q0).astype(F32)
        m_old = m_ref[...]
        m_new = jnp.maximum(m_old, jnp.max(s, axis=0, keepdims=True) + shift)
        m_ref[...] = m_new
        return jnp.exp2(m_old - m_new), jnp.exp2(s - (m_new - shift)).astype(BF16)

    def accumulate(j, alpha, p):
        v_aug = jnp.concatenate([vT_ref[j], ones_rows], axis=0)
        acc_ref[...] = alpha * acc_ref[...] + jnp.dot(v_aug, p, preferred_element_type=F32)

    m_ref[...] = jnp.full(m_ref.shape, NEG_BIG, F32)
    acc_ref[...] = jnp.zeros_like(acc_ref)
    sa_ref[...] = scores(0)
    odd = lax.rem(qi, 2)

    @pl.when(odd == 1)
    def _():
        s_next = scores(1)
        alpha, p = softmax(sa_ref, 0, False)
        sa_ref[...] = s_next
        p_ref[...] = p
        alpha_ref[...] = alpha

    @pl.when(odd == 0)
    def _():
        p_ref[...] = jnp.zeros_like(p_ref)
        alpha_ref[...] = jnp.ones_like(alpha_ref)

    def pair(t, carry):
        b0 = odd + 2 * t
        accumulate(jnp.maximum(b0 - 1, 0), alpha_ref[...], p_ref[...])
        sb_ref[...] = scores(b0 + 1)
        alpha, p = softmax(sa_ref, b0, False)
        accumulate(b0, alpha, p)
        sa_ref[...] = scores(b0 + 2)
        alpha, p = softmax(sb_ref, b0 + 1, False)
        p_ref[...] = p
        alpha_ref[...] = alpha
        return carry

    lax.fori_loop(0, (qi - odd) // 2, pair, 0)
    accumulate(jnp.maximum(qi - 1, 0), alpha_ref[...], p_ref[...])
    alpha, p = softmax(sa_ref, qi, True)
    accumulate(qi, alpha, p)

    acc = acc_ref[...]
    o_all = acc[:HEAD_V] / acc[HEAD_V:HEAD_V + 1]
    lam = _diff_lambda(lamv_ref[...], lam0)
    o = o_all[:, :bq] - lam * o_all[:, bq:]
    ms = jnp.mean(o * o, axis=0, keepdims=True)
    o = o * lax.rsqrt(ms + RMS_EPS) * sg_ref[...] * (1.0 - lam0)
    o_ref[...] = o.T.astype(BF16)


def _attn_prompt_call(slopes_l2, q, kb, vT, kbias, lamv, sg, lam0):
    t = q.shape[0]
    bq, bk = ATTN_BQ, ATTN_BK
    assert bq == bk
    nk = t // bk
    acc_rows = HEAD_V + DENOM_ROWS
    return pl.pallas_call(
        functools.partial(_attn_prompt_kernel, bq=bq, bk=bk, lam0=lam0),
        grid=(N_HEADS, t // bq),
        in_specs=[pl.BlockSpec(memory_space=pltpu.SMEM),
                  pl.BlockSpec((bq, KEY_DIM), lambda h, i: (i, h)),
                  pl.BlockSpec((t, KEY_DIM), lambda h, i: (0, h)),
                  pl.BlockSpec((nk, HEAD_V, bk), lambda h, i: (0, h, 0)),
                  pl.BlockSpec((1, bk, KEY_DIM), lambda h, i: (h, 0, 0)),
                  pl.BlockSpec(lamv.shape, lambda h, i: (0, 0)),
                  pl.BlockSpec(sg.shape, lambda h, i: (0, 0))],
        out_specs=pl.BlockSpec((bq, HEAD_V), lambda h, i: (i, h)),
        out_shape=jax.ShapeDtypeStruct((t, D_MODEL), BF16),
        scratch_shapes=[pltpu.VMEM((bk, 2 * bq), F32), pltpu.VMEM((bk, 2 * bq), F32),
                        pltpu.VMEM((bk, 2 * bq), BF16), pltpu.VMEM((1, 2 * bq), F32),
                        pltpu.VMEM((1, 2 * bq), F32), pltpu.VMEM((acc_rows, 2 * bq), F32)],
        compiler_params=_cparams("arbitrary", "arbitrary"),
        name="diff_attn_prompt",
    )(slopes_l2, q, kb, vT, kbias, lamv, sg)


def _attn_sample_kernel(pt_ref, *refs, n_pages, page, ds, lam0):
    k_refs = refs[:n_pages]
    v_refs = refs[n_pages:2 * n_pages]
    (qbd_ref, bias_ref, slope_ref, knew_ref, vnew_ref, bnew_ref, lamv_ref, sg_ref,
     o_ref, kc_ref, vc_ref, m_ref, l_ref, acc_ref) = refs[2 * n_pages:]
    c = pl.program_id(1)
    nc = pl.num_programs(1)
    chunk = n_pages * page

    @pl.when(c == 0)
    def _():
        m_ref[...] = jnp.full(m_ref.shape, NEG_BIG, F32)
        l_ref[...] = jnp.zeros_like(l_ref)
        acc_ref[...] = jnp.zeros_like(acc_ref)

    def update(s, shift, v_bf16):
        m_old = m_ref[...]
        m_new = jnp.maximum(m_old, jnp.max(s, axis=1, keepdims=True) + shift)
        alpha = jnp.exp2(m_old - m_new)
        p = jnp.exp2(s - (m_new - shift))
        l_ref[...] = alpha * l_ref[...] + jnp.sum(p, axis=1, keepdims=True)
        acc_ref[...] = alpha * acc_ref[...] + jnp.dot(p.astype(BF16), v_bf16,
                                                      preferred_element_type=F32)
        m_ref[...] = m_new

    for i in range(n_pages):
        for hd in range(N_HEADS):
            rows_h = pl.ds(hd, page, stride=N_HEADS)
            kc_ref[i * page:(i + 1) * page, hd * KEY_DIM:(hd + 1) * KEY_DIM] = k_refs[i][rows_h, :].astype(BF16)
            vc_ref[i * page:(i + 1) * page, hd * HEAD_V:(hd + 1) * HEAD_V] = v_refs[i][rows_h, :].astype(BF16)
    sT = jnp.dot(kc_ref[...], qbd_ref[0], preferred_element_type=F32)
    s = sT.T + bias_ref[...]
    past = nc * chunk
    shift = slope_ref[...] * (c * chunk - past).astype(F32)
    update(s, shift, vc_ref[...])

    @pl.when(c == nc - 1)
    def _():
        sTn = jnp.dot(knew_ref[0], qbd_ref[0], preferred_element_type=F32)
        sn = sTn.T + bnew_ref[...]
        update(sn, jnp.zeros_like(slope_ref[...]), vnew_ref[0])

        lam = _diff_lambda(lamv_ref[...], lam0)
        acc = acc_ref[...]
        l = l_ref[...]
        rows = 2 * ds
        for hd in range(N_HEADS):
            blk = acc[hd * rows:(hd + 1) * rows, hd * HEAD_V:(hd + 1) * HEAD_V] / l[hd * rows:(hd + 1) * rows]
            o = blk[0:ds] - lam * blk[ds:rows]
            ms = jnp.mean(o * o, axis=1, keepdims=True)
            o = o * lax.rsqrt(ms + RMS_EPS) * sg_ref[...] * (1.0 - lam0)
            o_ref[0, 0:ds, hd * HEAD_V:(hd + 1) * HEAD_V] = o
            o_ref[0, ds:rows, hd * HEAD_V:(hd + 1) * HEAD_V] = jnp.zeros((rows - ds, HEAD_V), F32)


def _attn_sample_call(page_table, cache_k, cache_v, qbd, bias, slope_col, knew, vnew, bnew, lamv, sg_row,
                      lam0, ds):
    nb, n_pages_total = page_table.shape
    page = cache_k.shape[1] // N_HEADS
    n_pages = PAGES_PER_STEP
    chunk = n_pages * page
    rows = 2 * ds
    page_spec = lambda i: pl.BlockSpec((None, page * N_HEADS, KEY_DIM),
                                       lambda b, c, pt, i=i: (pt[b, c * n_pages + i], 0, 0))
    const2 = lambda a: pl.BlockSpec(a.shape, lambda b, c, pt: (0, 0))
    per_seq = lambda a: pl.BlockSpec((1,) + a.shape[1:], lambda b, c, pt: (b, 0, 0))
    grid_spec = pltpu.PrefetchScalarGridSpec(
        num_scalar_prefetch=1,
        grid=(nb, n_pages_total // n_pages),
        in_specs=([page_spec(i) for i in range(n_pages)] + [page_spec(i) for i in range(n_pages)]
                  + [per_seq(qbd), const2(bias), const2(slope_col), per_seq(knew), per_seq(vnew),
                     const2(bnew), const2(lamv), const2(sg_row)]),
        out_specs=pl.BlockSpec((1, rows, D_MODEL), lambda b, c, pt: (b, 0, 0)),
        scratch_shapes=[pltpu.VMEM((chunk, D_MODEL), BF16), pltpu.VMEM((chunk, D_MODEL), BF16),
                        pltpu.VMEM((LANES, 1), F32), pltpu.VMEM((LANES, 1), F32),
                        pltpu.VMEM((LANES, D_MODEL), F32)])
    return pl.pallas_call(
        functools.partial(_attn_sample_kernel, n_pages=n_pages, page=page, ds=ds, lam0=lam0),
        grid_spec=grid_spec,
        out_shape=jax.ShapeDtypeStruct((nb, rows, D_MODEL), F32),
        compiler_params=_cparams("arbitrary", "arbitrary"),
        name="diff_attn_sample",
    )(page_table, *([cache_k] * n_pages), *([cache_v] * n_pages), qbd, bias, slope_col, knew, vnew,
      bnew, lamv, sg_row)


def _oproj_kernel(o_ref, x_ref, mod_ref, wo_ref, gffn_ref, wrT_ref, rb_ref, x1_ref, h2_ref, gT_ref):
    mod6 = _split_mod(mod_ref[...])
    y = jnp.dot(o_ref[...], wo_ref[...], preferred_element_type=F32)
    _ffn_prologue(x_ref[...] + mod6[2] * y, mod6, gffn_ref, wrT_ref, rb_ref, x1_ref, h2_ref, gT_ref)


def _oproj_call(o, x, mod, wo, gffn, wrT, rb):
    t = x.shape[0]
    tm = min(ROW_TILE, t)
    const = lambda shape: pl.BlockSpec(shape, lambda i: (0,) * len(shape))
    row = lambda: pl.BlockSpec((tm, D_MODEL), lambda i: (i, 0))
    mod_spec = const((1, 6 * D_MODEL)) if mod.shape[0] == 1 else pl.BlockSpec((tm, 6 * D_MODEL), lambda i: (i, 0))
    return pl.pallas_call(
        _oproj_kernel,
        grid=(t // tm,),
        in_specs=[row(), row(), mod_spec, const(wo.shape), const((1, D_MODEL)),
                  const((N_EXPERTS, D_MODEL)), const((N_EXPERTS, 1))],
        out_specs=[row(), row(), pl.BlockSpec((N_EXPERTS, tm), lambda i: (0, i))],
        out_shape=[jax.ShapeDtypeStruct((t, D_MODEL), F32),
                   jax.ShapeDtypeStruct((t, D_MODEL), BF16),
                   jax.ShapeDtypeStruct((N_EXPERTS, t), F32)],
        compiler_params=_cparams("arbitrary"),
        name="attn_out_proj",
    )(o, x, mod, wo, gffn, wrT, rb)


def kernel(x_prompt, x_sample, state_pool, cache_k, cache_v, page_table, c_prompt, c_sample,
           w_mod, b_mod, norm_mix_g, norm_ffn_g, pool_w, pool_scale, w_qkv, q_norm_g, k_norm_g,
           lambda_q1, lambda_k1, lambda_q2, lambda_k2, subln_g, w_o, w_router, router_bias,
           w_gate, w_up, w_down):
    assert DEPTH == 2 and x_prompt.shape[0] == 1
    seq = x_prompt.shape[1]
    nb, ds, _ = x_sample.shape
    n_phys, page = cache_k.shape[1], cache_k.shape[2]
    past = page_table.shape[1] * page
    ts = nb * ds

    c_all = jnp.concatenate([c_prompt, jnp.zeros((7, D_MODEL), F32), c_sample], axis=0)
    mods = _mod_call(c_all, w_mod, b_mod)
    mod_p = mods[:, 0:1]
    mod_s = jnp.tile(mods[:, 8:8 + nb], (1, ds, 1))

    pool_w_b = pool_w.astype(BF16)
    wgu = jnp.concatenate([w_gate, w_up], axis=-1).astype(BF16)
    wd = w_down.astype(BF16)
    wqkv = w_qkv.astype(BF16)
    wo = w_o.astype(BF16)
    wrT = w_router.T
    rb = router_bias.reshape(N_EXPERTS, 1)
    gq = jnp.tile(q_norm_g.reshape(1, 1, KEY_DIM), (1, N_HEADS, 1)).reshape(-1, 1, D_MODEL)
    gk = jnp.tile(k_norm_g.reshape(1, 1, KEY_DIM), (1, N_HEADS, 1)).reshape(-1, 1, D_MODEL)
    seg_id = jnp.arange(D_MODEL) // HEAD_QK
    seg = (seg_id[:, None] == jnp.arange(LANES)[None, :]).astype(BF16)
    segT = seg.T
    slopes = 2.0 ** (-8.0 * jnp.arange(1, N_HEADS + 1, dtype=F32) / N_HEADS)
    slopes_l2 = slopes * LOG2E

    xs = x_sample.transpose(1, 0, 2).reshape(ts, D_MODEL)
    state = state_pool.transpose(0, 2, 1, 3)

    x1p, h2p, gTp, pst = _pool_prompt_call(x_prompt[0], mod_p[0], norm_mix_g[0:1], norm_ffn_g[0:1],
                                           pool_w_b[0], pool_scale[0:1], wrT, rb)
    x1s, h2s, gTs, pool_s = _pool_sample_call(xs, state[0], mod_s[0], norm_mix_g[0:1], norm_ffn_g[0:1],
                                              pool_w_b[0], pool_scale[0:1], wrT, rb, nb, ds)
    xp = _moe_call(h2p, gTp.T, x1p, mod_p[0][:, 5 * D_MODEL:], wgu[0], wd[0])
    xs = _moe_call(h2s, gTs.T, x1s, mod_s[0][:, 5 * D_MODEL:], wgu[0], wd[0])

    lam0 = _lambda_init(1)
    lamv = jnp.stack([lambda_q1[0], lambda_k1[0], lambda_q2[0], lambda_k2[0]])
    qb, k32, v32, kb, vT = _qkv_call(xp, mod_p[1], norm_mix_g[1:2], wqkv[0], gq[0], gk[0], seg, segT, True)
    rest = slopes_l2[:, None] * jnp.arange(ATTN_BK, dtype=F32)[None, :]
    parts = []
    for _ in range(ALIBI_PARTS):
        part = _truncate_to_bf16(rest)
        parts.append(part)
        rest = rest - part
    kbias = jnp.pad(jnp.stack(parts, axis=-1).astype(BF16), ((0, 0), (0, 0), (0, KEY_DIM - ALIBI_PARTS)))
    o_p = _attn_prompt_call(slopes_l2, qb, kb, vT, kbias, lamv, subln_g[0].reshape(HEAD_V, 1), lam0)
    x1p, h2p, gTp = _oproj_call(o_p, xp, mod_p[1], wo[0], norm_ffn_g[1:2], wrT, rb)
    yp = _moe_call(h2p, gTp.T, x1p, mod_p[1][:, 5 * D_MODEL:], wgu[1], wd[1])

    qs, ks32, vs32 = _qkv_call(xs, mod_s[1], norm_mix_g[1:2], wqkv[0], gq[0], gk[0], seg, segT, False)
    rows = 2 * ds
    q_seq = qs.astype(F32).reshape(ds, nb, D_MODEL).transpose(1, 2, 0)
    colmask = (jnp.arange(D_MODEL)[:, None] // HEAD_QK) == (jnp.arange(N_HEADS * rows)[None, :] // ds)
    qbd = jnp.where(colmask[None], jnp.tile(q_seq, (1, 1, 2 * N_HEADS)), 0.0)
    qbd = jnp.pad(qbd, ((0, 0), (0, 0), (0, LANES - N_HEADS * rows))).astype(BF16)
    row_slope = jnp.pad(jnp.repeat(slopes_l2, rows), (0, LANES - N_HEADS * rows))
    chunk = PAGES_PER_STEP * page
    bias_s = row_slope[:, None] * jnp.arange(chunk, dtype=F32)[None, :]
    jn = jnp.arange(LANES)[None, :]
    tok = (jnp.arange(LANES) % ds)[:, None]
    bias_new = jnp.where(jnp.logical_and(jn <= tok, jn < ds), row_slope[:, None] * jn.astype(F32), -jnp.inf)
    to_seq = lambda a: jnp.pad(a.reshape(ds, nb, D_MODEL).transpose(1, 0, 2),
                               ((0, 0), (0, LANES - ds), (0, 0))).astype(BF16)
    o_s = _attn_sample_call(page_table, cache_k.reshape(-1, page * N_HEADS, KEY_DIM),
                            cache_v.reshape(-1, page * N_HEADS, HEAD_V), qbd, bias_s,
                            row_slope.reshape(LANES, 1), to_seq(ks32), to_seq(vs32), bias_new, lamv,
                            subln_g[0].reshape(1, HEAD_V), lam0, ds)
    o_s = o_s[:, :ds].transpose(1, 0, 2).reshape(ts, D_MODEL).astype(BF16)
    x1s, h2s, gTs = _oproj_call(o_s, xs, mod_s[1], wo[0], norm_ffn_g[1:2], wrT, rb)
    ys = _moe_call(h2s, gTs.T, x1s, mod_s[1][:, 5 * D_MODEL:], wgu[1], wd[1])

    seq_major = lambda a: a.reshape(ds, nb, D_MODEL).transpose(1, 0, 2)
    return (yp[None],
            seq_major(ys),
            pst[None, None, 1:],
            pool_s.transpose(1, 0, 2)[None],
            k32.reshape(1, 1, seq // page, page, N_HEADS, KEY_DIM),
            v32.reshape(1, 1, seq // page, page, N_HEADS, HEAD_V),
            seq_major(ks32).reshape(1, nb, ds, N_HEADS, KEY_DIM),
            seq_major(vs32).reshape(1, nb, ds, N_HEADS, HEAD_V))
```

```python
import functools
import math

import jax
import jax.numpy as jnp
from jax import lax
from jax.experimental import pallas as pl
from jax.experimental.pallas import tpu as pltpu

F32 = jnp.float32
BF16 = jnp.bfloat16
HIGHEST = lax.Precision.HIGHEST

D_MODEL = 1024
DEPTH = 2
POOL_WINDOWS = (2, 4, 8, 16)
POOL_GROUP = D_MODEL // len(POOL_WINDOWS)
POOL_BUF = max(POOL_WINDOWS) - 1
HALO = POOL_BUF + 1
N_HEADS = 8
HEAD_QK = 64
HEAD_V = 2 * HEAD_QK
KEY_DIM = 2 * HEAD_QK
N_EXPERTS = 16
EXPERTS_PER_GROUP = 4
N_GROUPS = N_EXPERTS // EXPERTS_PER_GROUP
D_EXPERT = D_MODEL // 2
RMS_EPS = 1e-6
LOG2E = 1.4426950408889634
LANES = 128
NEG_BIG = -1e30

ROW_TILE = 512
MOE_ROW_TILE = 1024
MOE_CHUNK = 160
ATTN_BQ = ROW_TILE
ATTN_BK = ROW_TILE
ALIBI_PARTS = 3
DENOM_ROWS = 16
PAGES_PER_STEP = 8
VMEM_LIMIT = 56 * 1024 * 1024


def _cparams(*sem):
    return pltpu.CompilerParams(dimension_semantics=sem, vmem_limit_bytes=VMEM_LIMIT)


def _lambda_init(layer):
    return 0.8 - 0.6 * math.exp(-0.3 * layer)


def _truncate_to_bf16(x):
    bits = lax.bitcast_convert_type(x, jnp.uint32) & jnp.uint32(0xFFFF0000)
    return lax.bitcast_convert_type(bits, F32)


def _rms(x, g):
    ms = jnp.mean(x * x, axis=-1, keepdims=True)
    return x * lax.rsqrt(ms + RMS_EPS) * g


def _modulate(x, g, shift, scale):
    return _rms(x, g) * (1.0 + scale) + shift


def _split_mod(mod):
    return [mod[:, k * D_MODEL:(k + 1) * D_MODEL] for k in range(6)]


def _route_gates(h, wrT_ref, rb_ref, gT_ref):
    logits = lax.dot_general(wrT_ref[...], h, (((1,), (1,)), ((), ())),
                             precision=HIGHEST, preferred_element_type=F32)
    mx = jnp.max(logits, axis=0, keepdims=True)
    ex = jnp.exp(logits - mx)
    probs = ex / jnp.sum(ex, axis=0, keepdims=True)
    sel = probs + rb_ref[...]
    srow = [sel[e:e + 1] for e in range(N_EXPERTS)]
    prow = [probs[e:e + 1] for e in range(N_EXPERTS)]

    gscore = []
    for g in range(N_GROUPS):
        r = srow[g * EXPERTS_PER_GROUP:(g + 1) * EXPERTS_PER_GROUP]
        best = None
        for a in range(EXPERTS_PER_GROUP):
            for b in range(a + 1, EXPERTS_PER_GROUP):
                s = r[a] + r[b]
                best = s if best is None else jnp.maximum(best, s)
        gscore.append(best)
    gbest = functools.reduce(jnp.maximum, gscore)
    in_grp, taken = [], None
    for g in range(N_GROUPS):
        hit = gscore[g] == gbest
        if taken is not None:
            hit = jnp.logical_and(hit, jnp.logical_not(taken))
        taken = hit if taken is None else jnp.logical_or(taken, hit)
        in_grp.append(hit)

    def pick(rows, k):
        out = rows[(N_GROUPS - 1) * EXPERTS_PER_GROUP + k]
        for g in range(N_GROUPS - 2, -1, -1):
            out = jnp.where(in_grp[g], rows[g * EXPERTS_PER_GROUP + k], out)
        return out

    v = [pick(srow, k) for k in range(EXPERTS_PER_GROUP)]
    p = [pick(prow, k) for k in range(EXPERTS_PER_GROUP)]

    def first_hits(vals, target):
        hits, seen = [], None
        for x in vals:
            hit = x == target
            if seen is not None:
                hit = jnp.logical_and(hit, jnp.logical_not(seen))
            seen = hit if seen is None else jnp.logical_or(seen, hit)
            hits.append(hit)
        return hits

    top1 = first_hits(v, functools.reduce(jnp.maximum, v))
    rest = [jnp.where(top1[k], -jnp.inf, v[k]) for k in range(EXPERTS_PER_GROUP)]
    top2 = first_hits(rest, functools.reduce(jnp.maximum, rest))
    p1 = functools.reduce(jnp.add, [jnp.where(top1[k], p[k], 0.0) for k in range(EXPERTS_PER_GROUP)])
    p2 = functools.reduce(jnp.add, [jnp.where(top2[k], p[k], 0.0) for k in range(EXPERTS_PER_GROUP)])
    den = p1 + p2
    w1, w2 = p1 / den, p2 / den
    for g in range(N_GROUPS):
        for k in range(EXPERTS_PER_GROUP):
            val = jnp.where(top1[k], w1, jnp.where(top2[k], w2, 0.0))
            e = g * EXPERTS_PER_GROUP + k
            gT_ref[e:e + 1, :] = jnp.where(in_grp[g], val, 0.0)


def _ffn_prologue(x1, mod6, gffn_ref, wrT_ref, rb_ref, x1_ref, h2_ref, gT_ref):
    x1_ref[...] = x1
    h2 = _modulate(x1, gffn_ref[...], mod6[3], mod6[4])
    h2_ref[...] = h2.astype(BF16)
    _route_gates(h2, wrT_ref, rb_ref, gT_ref)


def _mod_kernel(c_ref, w_ref, b_ref, o_ref):
    c = c_ref[...]
    a = c * jax.nn.sigmoid(c)
    o_ref[0] = jnp.dot(a, w_ref[0], precision=HIGHEST, preferred_element_type=F32) + b_ref[0]


def _mod_call(c_all, w_mod, b_mod):
    rows = c_all.shape[0]
    tn = 1536
    return pl.pallas_call(
        _mod_kernel,
        grid=(DEPTH, 6 * D_MODEL // tn),
        in_specs=[pl.BlockSpec((rows, D_MODEL), lambda i, j: (0, 0)),
                  pl.BlockSpec((1, D_MODEL, tn), lambda i, j: (i, 0, j)),
                  pl.BlockSpec((1, 1, tn), lambda i, j: (i, 0, j))],
        out_specs=pl.BlockSpec((1, rows, tn), lambda i, j: (i, 0, j)),
        out_shape=jax.ShapeDtypeStruct((DEPTH, rows, 6 * D_MODEL), F32),
        compiler_params=_cparams("arbitrary", "arbitrary"),
        name="adaln_mod",
    )(c_all, w_mod, b_mod.reshape(DEPTH, 1, 6 * D_MODEL))


def _pool_prompt_kernel(x_ref, mod_ref, gmix_ref, gffn_ref, pw_ref, ps_ref, wrT_ref, rb_ref,
                        x1_ref, h2_ref, gT_ref, pst_ref, ext_ref, *, tm):
    i = pl.program_id(0)

    @pl.when(i == 0)
    def _():
        ext_ref[0:HALO, :] = jnp.zeros((HALO, D_MODEL), F32)

    x = x_ref[...]
    mod6 = _split_mod(mod_ref[...])
    ext_ref[HALO:HALO + tm, :] = _modulate(x, gmix_ref[...], mod6[0], mod6[1])
    pos = i * tm + lax.broadcasted_iota(jnp.int32, (tm, 1), 0)
    ys = []
    for g, w in enumerate(POOL_WINDOWS):
        c0 = g * POOL_GROUP
        cur = ext_ref[HALO:HALO + tm, c0:c0 + POOL_GROUP]
        acc = cur
        for k in range(1, w):
            acc = acc + ext_ref[HALO - k:HALO - k + tm, c0:c0 + POOL_GROUP]
        cnt = jnp.minimum(pos + 1, w).astype(F32)
        d = acc / cnt - cur
        ys.append(jnp.dot(d.astype(BF16), pw_ref[g], preferred_element_type=F32))
    y = jnp.concatenate(ys, axis=1) * ps_ref[...]
    tail = ext_ref[tm:tm + HALO, :]
    ext_ref[0:HALO, :] = tail
    pst_ref[...] = tail
    _ffn_prologue(x + mod6[2] * y, mod6, gffn_ref, wrT_ref, rb_ref, x1_ref, h2_ref, gT_ref)


def _pool_prompt_call(x, mod, gmix, gffn, pool_w, pool_scale, wrT, rb):
    t = x.shape[0]
    tm = ROW_TILE
    const = lambda shape: pl.BlockSpec(shape, lambda i: (0,) * len(shape))
    return pl.pallas_call(
        functools.partial(_pool_prompt_kernel, tm=tm),
        grid=(t // tm,),
        in_specs=[pl.BlockSpec((tm, D_MODEL), lambda i: (i, 0)),
                  const((1, 6 * D_MODEL)), const((1, D_MODEL)), const((1, D_MODEL)),
                  const(pool_w.shape), const((1, D_MODEL)),
                  const((N_EXPERTS, D_MODEL)), const((N_EXPERTS, 1))],
        out_specs=[pl.BlockSpec((tm, D_MODEL), lambda i: (i, 0)),
                   pl.BlockSpec((tm, D_MODEL), lambda i: (i, 0)),
                   pl.BlockSpec((N_EXPERTS, tm), lambda i: (0, i)),
                   const((HALO, D_MODEL))],
        out_shape=[jax.ShapeDtypeStruct((t, D_MODEL), F32),
                   jax.ShapeDtypeStruct((t, D_MODEL), BF16),
                   jax.ShapeDtypeStruct((N_EXPERTS, t), F32),
                   jax.ShapeDtypeStruct((HALO, D_MODEL), F32)],
        scratch_shapes=[pltpu.VMEM((tm + HALO, D_MODEL), F32)],
        compiler_params=_cparams("arbitrary"),
        name="pool_mixer_prompt",
    )(x, mod, gmix, gffn, pool_w, pool_scale, wrT, rb)


def _pool_sample_kernel(x_ref, st_ref, mod_ref, gmix_ref, gffn_ref, pw_ref, ps_ref, wrT_ref, rb_ref,
                        x1_ref, h2_ref, gT_ref, pool_ref, *, nb, ds):
    x = x_ref[...]
    mod6 = _split_mod(mod_ref[...])
    hs = _modulate(x, gmix_ref[...], mod6[0], mod6[1])
    slabs = [st_ref[r] for r in range(POOL_BUF)] + [hs[t * nb:(t + 1) * nb] for t in range(ds)]
    for r in range(POOL_BUF):
        pool_ref[r] = slabs[ds + r]
    ys = []
    for g, w in enumerate(POOL_WINDOWS):
        c0 = g * POOL_GROUP
        ds_rows = []
        for t in range(ds):
            p = POOL_BUF + t
            acc = slabs[p][:, c0:c0 + POOL_GROUP]
            for k in range(1, w):
                acc = acc + slabs[p - k][:, c0:c0 + POOL_GROUP]
            ds_rows.append(acc / float(w) - slabs[p][:, c0:c0 + POOL_GROUP])
        d = jnp.concatenate(ds_rows, axis=0)
        ys.append(jnp.dot(d.astype(BF16), pw_ref[g], preferred_element_type=F32))
    y = jnp.concatenate(ys, axis=1) * ps_ref[...]
    _ffn_prologue(x + mod6[2] * y, mod6, gffn_ref, wrT_ref, rb_ref, x1_ref, h2_ref, gT_ref)


def _pool_sample_call(x, state, mod, gmix, gffn, pool_w, pool_scale, wrT, rb, nb, ds):
    t = x.shape[0]
    full = lambda a: pl.BlockSpec(a.shape, lambda: (0,) * a.ndim)
    args = (x, state, mod, gmix, gffn, pool_w, pool_scale, wrT, rb)
    out_shape = [jax.ShapeDtypeStruct((t, D_MODEL), F32),
                 jax.ShapeDtypeStruct((t, D_MODEL), BF16),
                 jax.ShapeDtypeStruct((N_EXPERTS, t), F32),
                 jax.ShapeDtypeStruct((POOL_BUF, nb, D_MODEL), F32)]
    return pl.pallas_call(
        functools.partial(_pool_sample_kernel, nb=nb, ds=ds),
        in_specs=[full(a) for a in args],
        out_specs=[full(s) for s in out_shape],
        out_shape=out_shape,
        compiler_params=pltpu.CompilerParams(vmem_limit_bytes=VMEM_LIMIT),
        name="pool_mixer_sample",
    )(*args)


def _moe_kernel(h_ref, gates_ref, x_ref, g2_ref, wgu_ref, wd_ref, o_ref, acc_ref):
    e = pl.program_id(1)

    @pl.when(e == 0)
    def _():
        acc_ref[...] = jnp.zeros_like(acc_ref)

    gu = jnp.dot(h_ref[...], wgu_ref[0], preferred_element_type=F32)
    gate_in = gu[:, :D_EXPERT]
    a = gate_in * jax.nn.sigmoid(gate_in) * gu[:, D_EXPERT:]
    gates = gates_ref[...]
    lane = lax.broadcasted_iota(jnp.int32, gates.shape, 1)
    gcol = jnp.sum(jnp.where(lane == e, gates, 0.0), axis=1, keepdims=True)
    acc_ref[...] += jnp.dot((a * gcol).astype(BF16), wd_ref[0], preferred_element_type=F32)

    @pl.when(e == N_EXPERTS - 1)
    def _():
        o_ref[...] = x_ref[...] + g2_ref[...] * acc_ref[...]


def _moe_call(h, gates, x, g2, wgu, wd):
    t = h.shape[0]
    tm = min(MOE_ROW_TILE, t)
    g2_rows = g2.shape[0]
    g2_spec = (pl.BlockSpec((1, D_MODEL), lambda i, e: (0, 0)) if g2_rows == 1
               else pl.BlockSpec((tm, D_MODEL), lambda i, e: (i, 0)))
    return pl.pallas_call(
        _moe_kernel,
        grid=(t // tm, N_EXPERTS),
        in_specs=[pl.BlockSpec((tm, D_MODEL), lambda i, e: (i, 0)),
                  pl.BlockSpec((tm, N_EXPERTS), lambda i, e: (i, 0)),
                  pl.BlockSpec((tm, D_MODEL), lambda i, e: (i, 0)),
                  g2_spec,
                  pl.BlockSpec((1, D_MODEL, 2 * D_EXPERT), lambda i, e: (e, 0, 0)),
                  pl.BlockSpec((1, D_EXPERT, D_MODEL), lambda i, e: (e, 0, 0))],
        out_specs=pl.BlockSpec((tm, D_MODEL), lambda i, e: (i, 0)),
        out_shape=jax.ShapeDtypeStruct((t, D_MODEL), F32),
        scratch_shapes=[pltpu.VMEM((tm, D_MODEL), F32)],
        compiler_params=_cparams("arbitrary", "arbitrary"),
        name="moe_dense",
    )(h, gates, x, g2, wgu, wd)


def _moe_routed_kernel(cnt_ref, h_ref, gates_ref, gatesT_ref, x_ref, g2_ref, before_ref, beforeT_ref,
                       wgu_ref, wd_ref, o_ref, acc_ref, rank_ref, rankT_ref, *, chunk):
    i = pl.program_id(0)
    e = pl.program_id(1)
    tm = h_ref.shape[0]

    @pl.when(e == 0)
    def _():
        acc_ref[...] = jnp.zeros_like(acc_ref)
        rankT_ref[...] = jnp.dot(jnp.where(gatesT_ref[...] > 0.0, 1.0, 0.0).astype(BF16), before_ref[...],
                                 preferred_element_type=F32)
        rank_ref[...] = jnp.dot(beforeT_ref[...], jnp.where(gates_ref[...] > 0.0, 1.0, 0.0).astype(BF16),
                                preferred_element_type=F32)

    g_row = gatesT_ref[pl.ds(e, 1), :]
    rank_row = rankT_ref[pl.ds(e, 1), :]
    pick = lax.broadcasted_iota(jnp.int32, (tm, N_EXPERTS), 1) == e
    g_col = jnp.sum(jnp.where(pick, gates_ref[...], 0.0), axis=1, keepdims=True)
    rank_col = jnp.sum(jnp.where(pick, rank_ref[...], 0.0), axis=1, keepdims=True)

    def body(c, carry):
        base = (c * chunk).astype(F32)
        slot_r = base + lax.broadcasted_iota(jnp.int32, (chunk, 1), 0).astype(F32)
        slot_c = base + lax.broadcasted_iota(jnp.int32, (1, chunk), 1).astype(F32)
        take = jnp.logical_and(rank_row == slot_r, g_row > 0.0)
        put = jnp.logical_and(rank_col == slot_c, g_col > 0.0)
        xg = jnp.dot(jnp.where(take, 1.0, 0.0).astype(BF16), h_ref[...],
                     preferred_element_type=F32).astype(BF16)
        gu = jnp.dot(xg, wgu_ref[0], preferred_element_type=F32)
        gate_in = gu[:, :D_EXPERT]
        a = gate_in * jax.nn.sigmoid(gate_in) * gu[:, D_EXPERT:]
        g_sel = jnp.sum(jnp.where(take, g_row, 0.0), axis=1, keepdims=True)
        y = jnp.dot((a * g_sel).astype(BF16), wd_ref[0], preferred_element_type=F32)
        acc_ref[...] += jnp.dot(jnp.where(put, 1.0, 0.0).astype(BF16), y.astype(BF16),
                                preferred_element_type=F32)
        return carry

    lax.fori_loop(0, (cnt_ref[e, i] + chunk - 1) // chunk, body, 0)

    @pl.when(e == N_EXPERTS - 1)
    def _():
        o_ref[...] = x_ref[...] + g2_ref[...] * acc_ref[...]


def _moe_routed_call(h, gatesT, x, g2, wgu, wd):
    t = h.shape[0]
    tm = MOE_ROW_TILE
    n_tiles = t // tm
    gates = gatesT.T
    counts = jnp.sum((gatesT > 0.0).reshape(N_EXPERTS, n_tiles, tm), axis=-1).astype(jnp.int32)
    before = (jnp.arange(tm)[:, None] < jnp.arange(tm)[None, :]).astype(BF16)
    grid_spec = pltpu.PrefetchScalarGridSpec(
        num_scalar_prefetch=1,
        grid=(n_tiles, N_EXPERTS),
        in_specs=[pl.BlockSpec((tm, D_MODEL), lambda i, e, cnt: (i, 0)),
                  pl.BlockSpec((tm, N_EXPERTS), lambda i, e, cnt: (i, 0)),
                  pl.BlockSpec((N_EXPERTS, tm), lambda i, e, cnt: (0, i)),
                  pl.BlockSpec((tm, D_MODEL), lambda i, e, cnt: (i, 0)),
                  pl.BlockSpec((1, D_MODEL), lambda i, e, cnt: (0, 0)),
                  pl.BlockSpec((tm, tm), lambda i, e, cnt: (0, 0)),
                  pl.BlockSpec((tm, tm), lambda i, e, cnt: (0, 0)),
                  pl.BlockSpec((1, D_MODEL, 2 * D_EXPERT), lambda i, e, cnt: (e, 0, 0)),
                  pl.BlockSpec((1, D_EXPERT, D_MODEL), lambda i, e, cnt: (e, 0, 0))],
        out_specs=pl.BlockSpec((tm, D_MODEL), lambda i, e, cnt: (i, 0)),
        scratch_shapes=[pltpu.VMEM((tm, D_MODEL), F32), pltpu.VMEM((tm, N_EXPERTS), F32),
                        pltpu.VMEM((N_EXPERTS, tm), F32)])
    return pl.pallas_call(
        functools.partial(_moe_routed_kernel, chunk=MOE_CHUNK),
        grid_spec=grid_spec,
        out_shape=jax.ShapeDtypeStruct((t, D_MODEL), F32),
        compiler_params=_cparams("arbitrary", "arbitrary"),
        name="moe_routed",
    )(counts, h, gates, gatesT, x, g2, before, before.T, wgu, wd)


def _seg_norm(t, g_full, seg_ref, segT_ref):
    sq = t * t
    hi = sq.astype(BF16)
    lo = (sq - hi.astype(F32)).astype(BF16)
    ssq = (jnp.dot(hi, seg_ref[...], preferred_element_type=F32)
           + jnp.dot(lo, seg_ref[...], preferred_element_type=F32))
    inv = lax.rsqrt(ssq * (1.0 / HEAD_QK) + RMS_EPS)
    ihi = inv.astype(BF16)
    ilo = (inv - ihi.astype(F32)).astype(BF16)
    inv_full = (jnp.dot(ihi, segT_ref[...], preferred_element_type=F32)
                + jnp.dot(ilo, segT_ref[...], preferred_element_type=F32))
    return t * inv_full * g_full


def _qkv_kernel(x_ref, mod_ref, gmix_ref, w_ref, gq_ref, gk_ref, seg_ref, segT_ref, *out_refs,
                prompt):
    mod6 = _split_mod(mod_ref[...])
    h = _modulate(x_ref[...], gmix_ref[...], mod6[0], mod6[1]).astype(BF16)
    qkv = jnp.dot(h, w_ref[...], preferred_element_type=F32)
    q = _seg_norm(qkv[:, :D_MODEL], gq_ref[...], seg_ref, segT_ref)
    k = _seg_norm(qkv[:, D_MODEL:2 * D_MODEL], gk_ref[...], seg_ref, segT_ref)
    v = qkv[:, 2 * D_MODEL:]
    q_ref, k32_ref, v32_ref = out_refs[:3]
    q_ref[...] = (q * (HEAD_QK ** -0.5 * LOG2E)).astype(BF16)
    k32_ref[...] = k
    v32_ref[...] = v
    if prompt:
        kb_ref, vT_ref = out_refs[3:]
        kb_ref[...] = k.astype(BF16)
        vT_ref[0] = v.T.astype(BF16)


def _qkv_call(x, mod, gmix, wqkv, gq, gk, seg, segT, prompt):
    t = x.shape[0]
    tm = min(ROW_TILE, t)
    const = lambda shape: pl.BlockSpec(shape, lambda i: (0,) * len(shape))
    row = lambda: pl.BlockSpec((tm, D_MODEL), lambda i: (i, 0))
    mod_spec = const((1, 6 * D_MODEL)) if mod.shape[0] == 1 else pl.BlockSpec((tm, 6 * D_MODEL), lambda i: (i, 0))
    out_specs = [row(), row(), row()]
    out_shape = [jax.ShapeDtypeStruct((t, D_MODEL), BF16),
                 jax.ShapeDtypeStruct((t, D_MODEL), F32),
                 jax.ShapeDtypeStruct((t, D_MODEL), F32)]
    if prompt:
        out_specs += [row(), pl.BlockSpec((1, D_MODEL, tm), lambda i: (i, 0, 0))]
        out_shape += [jax.ShapeDtypeStruct((t, D_MODEL), BF16),
                      jax.ShapeDtypeStruct((t // tm, D_MODEL, tm), BF16)]
    return pl.pallas_call(
        functools.partial(_qkv_kernel, prompt=prompt),
        grid=(t // tm,),
        in_specs=[row(), mod_spec, const((1, D_MODEL)), const(wqkv.shape),
                  const((1, D_MODEL)), const((1, D_MODEL)), const(seg.shape), const(segT.shape)],
        out_specs=out_specs,
        out_shape=out_shape,
        compiler_params=_cparams("arbitrary"),
        name="qkv_prompt" if prompt else "qkv_sample",
    )(x, mod, gmix, wqkv, gq, gk, seg, segT)


def _diff_lambda(lamv, lam0):
    a = jnp.sum(lamv[0:1] * lamv[1:2], axis=1, keepdims=True)
    b = jnp.sum(lamv[2:3] * lamv[3:4], axis=1, keepdims=True)
    return jnp.exp(a) - jnp.exp(b) + lam0


def _attn_prompt_kernel(slope_ref, q_ref, k_ref, vT_ref, kbias_ref, lamv_ref, sg_ref, o_ref,
                        sa_ref, sb_ref, p_ref, m_ref, alpha_ref, acc_ref, *, bq, bk, lam0):
    hd = pl.program_id(0)
    qi = pl.program_id(1)
    slope = slope_ref[hd]
    q0 = qi * bq

    qT = q_ref[...].astype(F32).T
    row = lax.broadcasted_iota(jnp.int32, qT.shape, 0)
    row2 = lax.broadcasted_iota(jnp.int32, (KEY_DIM, 2 * bq), 0)
    qp = jnp.concatenate(
        [jnp.concatenate([jnp.where(row < HEAD_QK, qT, 0.0), jnp.where(row >= HEAD_QK, qT, 0.0)], axis=1),
         jnp.where(row2 < ALIBI_PARTS, 1.0, 0.0)], axis=0).astype(BF16)
    kbias = kbias_ref[0]
    ones_rows = jnp.ones((DENOM_ROWS, bk), BF16)

    def scores(j):
        kblk = k_ref[pl.ds(pl.multiple_of(j * bk, bk), bk), :]
        return jnp.dot(jnp.concatenate([kblk, kbias], axis=1), qp,
                       preferred_element_type=F32)

    def softmax(s_ref, j, masked):
        s = s_ref[...]
        if masked:
            kk = lax.broadcasted_iota(jnp.int32, s.shape, 0)
            qq = lax.broadcasted_iota(jnp.int32, s.shape, 1)
            s = jnp.where(kk <= jnp.where(qq >= bq, qq - bq, qq), s, -jnp.inf)
        shift = slope * (j * bk - q0).astype(F32)
        m_old = m_ref[...]
        m_new = jnp.maximum(m_old, jnp.max(s, axis=0, keepdims=True) + shift)
        m_ref[...] = m_new
        return jnp.exp2(m_old - m_new), jnp.exp2(s - (m_new - shift)).astype(BF16)

    def accumulate(j, alpha, p):
        v_aug = jnp.concatenate([vT_ref[j], ones_rows], axis=0)
        acc_ref[...] = alpha * acc_ref[...] + jnp.dot(v_aug, p, preferred_element_type=F32)

    m_ref[...] = jnp.full(m_ref.shape, NEG_BIG, F32)
    acc_ref[...] = jnp.zeros_like(acc_ref)
    sa_ref[...] = scores(0)
    odd = lax.rem(qi, 2)

    @pl.when(odd == 1)
    def _():
        s_next = scores(1)
        alpha, p = softmax(sa_ref, 0, False)
        sa_ref[...] = s_next
        p_ref[...] = p
        alpha_ref[...] = alpha

    @pl.when(odd == 0)
    def _():
        p_ref[...] = jnp.zeros_like(p_ref)
        alpha_ref[...] = jnp.ones_like(alpha_ref)

    def pair(t, carry):
        b0 = odd + 2 * t
        accumulate(jnp.maximum(b0 - 1, 0), alpha_ref[...], p_ref[...])
        sb_ref[...] = scores(b0 + 1)
        alpha, p = softmax(sa_ref, b0, False)
        accumulate(b0, alpha, p)
        sa_ref[...] = scores(b0 + 2)
        alpha, p = softmax(sb_ref, b0 + 1, False)
        p_ref[...] = p
        alpha_ref[...] = alpha
        return carry

    lax.fori_loop(0, (qi - odd) // 2, pair, 0)
    accumulate(jnp.maximum(qi - 1, 0), alpha_ref[...], p_ref[...])
    alpha, p = softmax(sa_ref, qi, True)
    accumulate(qi, alpha, p)

    acc = acc_ref[...]
    o_all = acc[:HEAD_V] / acc[HEAD_V:HEAD_V + 1]
    lam = _diff_lambda(lamv_ref[...], lam0)
    o = o_all[:, :bq] - lam * o_all[:, bq:]
    ms = jnp.mean(o * o, axis=0, keepdims=True)
    o = o * lax.rsqrt(ms + RMS_EPS) * sg_ref[...] * (1.0 - lam0)
    o_ref[...] = o.T.astype(BF16)


def _attn_prompt_call(slopes_l2, q, kb, vT, kbias, lamv, sg, lam0):
    t = q.shape[0]
    bq, bk = ATTN_BQ, ATTN_BK
    assert bq == bk
    nk = t // bk
    acc_rows = HEAD_V + DENOM_ROWS
    return pl.pallas_call(
        functools.partial(_attn_prompt_kernel, bq=bq, bk=bk, lam0=lam0),
        grid=(N_HEADS, t // bq),
        in_specs=[pl.BlockSpec(memory_space=pltpu.SMEM),
                  pl.BlockSpec((bq, KEY_DIM), lambda h, i: (i, h)),
                  pl.BlockSpec((t, KEY_DIM), lambda h, i: (0, h)),
                  pl.BlockSpec((nk, HEAD_V, bk), lambda h, i: (0, h, 0)),
                  pl.BlockSpec((1, bk, KEY_DIM), lambda h, i: (h, 0, 0)),
                  pl.BlockSpec(lamv.shape, lambda h, i: (0, 0)),
                  pl.BlockSpec(sg.shape, lambda h, i: (0, 0))],
        out_specs=pl.BlockSpec((bq, HEAD_V), lambda h, i: (i, h)),
        out_shape=jax.ShapeDtypeStruct((t, D_MODEL), BF16),
        scratch_shapes=[pltpu.VMEM((bk, 2 * bq), F32), pltpu.VMEM((bk, 2 * bq), F32),
                        pltpu.VMEM((bk, 2 * bq), BF16), pltpu.VMEM((1, 2 * bq), F32),
                        pltpu.VMEM((1, 2 * bq), F32), pltpu.VMEM((acc_rows, 2 * bq), F32)],
        compiler_params=_cparams("arbitrary", "arbitrary"),
        name="diff_attn_prompt",
    )(slopes_l2, q, kb, vT, kbias, lamv, sg)


def _attn_sample_kernel(pt_ref, *refs, n_pages, page, ds, lam0):
    k_refs = refs[:n_pages]
    v_refs = refs[n_pages:2 * n_pages]
    (qbd_ref, bias_ref, slope_ref, knew_ref, vnew_ref, bnew_ref, lamv_ref, sg_ref,
     o_ref, kc_ref, vc_ref, m_ref, l_ref, acc_ref) = refs[2 * n_pages:]
    c = pl.program_id(1)
    nc = pl.num_programs(1)
    chunk = n_pages * page

    @pl.when(c == 0)
    def _():
        m_ref[...] = jnp.full(m_ref.shape, NEG_BIG, F32)
        l_ref[...] = jnp.zeros_like(l_ref)
        acc_ref[...] = jnp.zeros_like(acc_ref)

    def update(s, shift, v_bf16):
        m_old = m_ref[...]
        m_new = jnp.maximum(m_old, jnp.max(s, axis=1, keepdims=True) + shift)
        alpha = jnp.exp2(m_old - m_new)
        p = jnp.exp2(s - (m_new - shift))
        l_ref[...] = alpha * l_ref[...] + jnp.sum(p, axis=1, keepdims=True)
        acc_ref[...] = alpha * acc_ref[...] + jnp.dot(p.astype(BF16), v_bf16,
                                                      preferred_element_type=F32)
        m_ref[...] = m_new

    for i in range(n_pages):
        for hd in range(N_HEADS):
            rows_h = pl.ds(hd, page, stride=N_HEADS)
            kc_ref[i * page:(i + 1) * page, hd * KEY_DIM:(hd + 1) * KEY_DIM] = k_refs[i][rows_h, :].astype(BF16)
            vc_ref[i * page:(i + 1) * page, hd * HEAD_V:(hd + 1) * HEAD_V] = v_refs[i][rows_h, :].astype(BF16)
    sT = jnp.dot(kc_ref[...], qbd_ref[0], preferred_element_type=F32)
    s = sT.T + bias_ref[...]
    past = nc * chunk
    shift = slope_ref[...] * (c * chunk - past).astype(F32)
    update(s, shift, vc_ref[...])

    @pl.when(c == nc - 1)
    def _():
        sTn = jnp.dot(knew_ref[0], qbd_ref[0], preferred_element_type=F32)
        sn = sTn.T + bnew_ref[...]
        update(sn, jnp.zeros_like(slope_ref[...]), vnew_ref[0])

        lam = _diff_lambda(lamv_ref[...], lam0)
        acc = acc_ref[...]
        l = l_ref[...]
        rows = 2 * ds
        for hd in range(N_HEADS):
            blk = acc[hd * rows:(hd + 1) * rows, hd * HEAD_V:(hd + 1) * HEAD_V] / l[hd * rows:(hd + 1) * rows]
            o = blk[0:ds] - lam * blk[ds:rows]
            ms = jnp.mean(o * o, axis=1, keepdims=True)
            o = o * lax.rsqrt(ms + RMS_EPS) * sg_ref[...] * (1.0 - lam0)
            o_ref[0, 0:ds, hd * HEAD_V:(hd + 1) * HEAD_V] = o
            o_ref[0, ds:rows, hd * HEAD_V:(hd + 1) * HEAD_V] = jnp.zeros((rows - ds, HEAD_V), F32)


def _attn_sample_call(page_table, cache_k, cache_v, qbd, bias, slope_col, knew, vnew, bnew, lamv, sg_row,
                      lam0, ds):
    nb, n_pages_total = page_table.shape
    page = cache_k.shape[1] // N_HEADS
    n_pages = PAGES_PER_STEP
    chunk = n_pages * page
    rows = 2 * ds
    page_spec = lambda i: pl.BlockSpec((None, page * N_HEADS, KEY_DIM),
                                       lambda b, c, pt, i=i: (pt[b, c * n_pages + i], 0, 0))
    const2 = lambda a: pl.BlockSpec(a.shape, lambda b, c, pt: (0, 0))
    per_seq = lambda a: pl.BlockSpec((1,) + a.shape[1:], lambda b, c, pt: (b, 0, 0))
    grid_spec = pltpu.PrefetchScalarGridSpec(
        num_scalar_prefetch=1,
        grid=(nb, n_pages_total // n_pages),
        in_specs=([page_spec(i) for i in range(n_pages)] + [page_spec(i) for i in range(n_pages)]
                  + [per_seq(qbd), const2(bias), const2(slope_col), per_seq(knew), per_seq(vnew),
                     const2(bnew), const2(lamv), const2(sg_row)]),
        out_specs=pl.BlockSpec((1, rows, D_MODEL), lambda b, c, pt: (b, 0, 0)),
        scratch_shapes=[pltpu.VMEM((chunk, D_MODEL), BF16), pltpu.VMEM((chunk, D_MODEL), BF16),
                        pltpu.VMEM((LANES, 1), F32), pltpu.VMEM((LANES, 1), F32),
                        pltpu.VMEM((LANES, D_MODEL), F32)])
    return pl.pallas_call(
        functools.partial(_attn_sample_kernel, n_pages=n_pages, page=page, ds=ds, lam0=lam0),
        grid_spec=grid_spec,
        out_shape=jax.ShapeDtypeStruct((nb, rows, D_MODEL), F32),
        compiler_params=_cparams("arbitrary", "arbitrary"),
        name="diff_attn_sample",
    )(page_table, *([cache_k] * n_pages), *([cache_v] * n_pages), qbd, bias, slope_col, knew, vnew,
      bnew, lamv, sg_row)


def _oproj_kernel(o_ref, x_ref, mod_ref, wo_ref, gffn_ref, wrT_ref, rb_ref, x1_ref, h2_ref, gT_ref):
    mod6 = _split_mod(mod_ref[...])
    y = jnp.dot(o_ref[...], wo_ref[...], preferred_element_type=F32)
    _ffn_prologue(x_ref[...] + mod6[2] * y, mod6, gffn_ref, wrT_ref, rb_ref, x1_ref, h2_ref, gT_ref)


def _oproj_call(o, x, mod, wo, gffn, wrT, rb):
    t = x.shape[0]
    tm = min(ROW_TILE, t)
    const = lambda shape: pl.BlockSpec(shape, lambda i: (0,) * len(shape))
    row = lambda: pl.BlockSpec((tm, D_MODEL), lambda i: (i, 0))
    mod_spec = const((1, 6 * D_MODEL)) if mod.shape[0] == 1 else pl.BlockSpec((tm, 6 * D_MODEL), lambda i: (i, 0))
    return pl.pallas_call(
        _oproj_kernel,
        grid=(t // tm,),
        in_specs=[row(), row(), mod_spec, const(wo.shape), const((1, D_MODEL)),
                  const((N_EXPERTS, D_MODEL)), const((N_EXPERTS, 1))],
        out_specs=[row(), row(), pl.BlockSpec((N_EXPERTS, tm), lambda i: (0, i))],
        out_shape=[jax.ShapeDtypeStruct((t, D_MODEL), F32),
                   jax.ShapeDtypeStruct((t, D_MODEL), BF16),
                   jax.ShapeDtypeStruct((N_EXPERTS, t), F32)],
        compiler_params=_cparams("arbitrary"),
        name="attn_out_proj",
    )(o, x, mod, wo, gffn, wrT, rb)


def kernel(x_prompt, x_sample, state_pool, cache_k, cache_v, page_table, c_prompt, c_sample,
           w_mod, b_mod, norm_mix_g, norm_ffn_g, pool_w, pool_scale, w_qkv, q_norm_g, k_norm_g,
           lambda_q1, lambda_k1, lambda_q2, lambda_k2, subln_g, w_o, w_router, router_bias,
           w_gate, w_up, w_down):
    assert DEPTH == 2 and x_prompt.shape[0] == 1
    seq = x_prompt.shape[1]
    nb, ds, _ = x_sample.shape
    n_phys, page = cache_k.shape[1], cache_k.shape[2]
    past = page_table.shape[1] * page
    ts = nb * ds

    c_all = jnp.concatenate([c_prompt, jnp.zeros((7, D_MODEL), F32), c_sample], axis=0)
    mods = _mod_call(c_all, w_mod, b_mod)
    mod_p = mods[:, 0:1]
    mod_s = jnp.tile(mods[:, 8:8 + nb], (1, ds, 1))

    pool_w_b = pool_w.astype(BF16)
    wgu = jnp.concatenate([w_gate, w_up], axis=-1).astype(BF16)
    wd = w_down.astype(BF16)
    wqkv = w_qkv.astype(BF16)
    wo = w_o.astype(BF16)
    wrT = w_router.T
    rb = router_bias.reshape(N_EXPERTS, 1)
    gq = jnp.tile(q_norm_g.reshape(1, 1, KEY_DIM), (1, N_HEADS, 1)).reshape(-1, 1, D_MODEL)
    gk = jnp.tile(k_norm_g.reshape(1, 1, KEY_DIM), (1, N_HEADS, 1)).reshape(-1, 1, D_MODEL)
    seg_id = jnp.arange(D_MODEL) // HEAD_QK
    seg = (seg_id[:, None] == jnp.arange(LANES)[None, :]).astype(BF16)
    segT = seg.T
    slopes = 2.0 ** (-8.0 * jnp.arange(1, N_HEADS + 1, dtype=F32) / N_HEADS)
    slopes_l2 = slopes * LOG2E

    xs = x_sample.transpose(1, 0, 2).reshape(ts, D_MODEL)
    state = state_pool.transpose(0, 2, 1, 3)

    x1p, h2p, gTp, pst = _pool_prompt_call(x_prompt[0], mod_p[0], norm_mix_g[0:1], norm_ffn_g[0:1],
                                           pool_w_b[0], pool_scale[0:1], wrT, rb)
    x1s, h2s, gTs, pool_s = _pool_sample_call(xs, state[0], mod_s[0], norm_mix_g[0:1], norm_ffn_g[0:1],
                                              pool_w_b[0], pool_scale[0:1], wrT, rb, nb, ds)
    xp = _moe_routed_call(h2p, gTp, x1p, mod_p[0][:, 5 * D_MODEL:], wgu[0], wd[0])
    xs = _moe_call(h2s, gTs.T, x1s, mod_s[0][:, 5 * D_MODEL:], wgu[0], wd[0])

    lam0 = _lambda_init(1)
    lamv = jnp.stack([lambda_q1[0], lambda_k1[0], lambda_q2[0], lambda_k2[0]])
    qb, k32, v32, kb, vT = _qkv_call(xp, mod_p[1], norm_mix_g[1:2], wqkv[0], gq[0], gk[0], seg, segT, True)
    rest = slopes_l2[:, None] * jnp.arange(ATTN_BK, dtype=F32)[None, :]
    parts = []
    for _ in range(ALIBI_PARTS):
        part = _truncate_to_bf16(rest)
        parts.append(part)
        rest = rest - part
    kbias = jnp.pad(jnp.stack(parts, axis=-1).astype(BF16), ((0, 0), (0, 0), (0, KEY_DIM - ALIBI_PARTS)))
    o_p = _attn_prompt_call(slopes_l2, qb, kb, vT, kbias, lamv, subln_g[0].reshape(HEAD_V, 1), lam0)
    x1p, h2p, gTp = _oproj_call(o_p, xp, mod_p[1], wo[0], norm_ffn_g[1:2], wrT, rb)
    yp = _moe_routed_call(h2p, gTp, x1p, mod_p[1][:, 5 * D_MODEL:], wgu[1], wd[1])

    qs, ks32, vs32 = _qkv_call(xs, mod_s[1], norm_mix_g[1:2], wqkv[0], gq[0], gk[0], seg, segT, False)
    rows = 2 * ds
    q_seq = qs.astype(F32).reshape(ds, nb, D_MODEL).transpose(1, 2, 0)
    colmask = (jnp.arange(D_MODEL)[:, None] // HEAD_QK) == (jnp.arange(N_HEADS * rows)[None, :] // ds)
    qbd = jnp.where(colmask[None], jnp.tile(q_seq, (1, 1, 2 * N_HEADS)), 0.0)
    qbd = jnp.pad(qbd, ((0, 0), (0, 0), (0, LANES - N_HEADS * rows))).astype(BF16)
    row_slope = jnp.pad(jnp.repeat(slopes_l2, rows), (0, LANES - N_HEADS * rows))
    chunk = PAGES_PER_STEP * page
    bias_s = row_slope[:, None] * jnp.arange(chunk, dtype=F32)[None, :]
    jn = jnp.arange(LANES)[None, :]
    tok = (jnp.arange(LANES) % ds)[:, None]
    bias_new = jnp.where(jnp.logical_and(jn <= tok, jn < ds), row_slope[:, None] * jn.astype(F32), -jnp.inf)
    to_seq = lambda a: jnp.pad(a.reshape(ds, nb, D_MODEL).transpose(1, 0, 2),
                               ((0, 0), (0, LANES - ds), (0, 0))).astype(BF16)
    o_s = _attn_sample_call(page_table, cache_k.reshape(-1, page * N_HEADS, KEY_DIM),
                            cache_v.reshape(-1, page * N_HEADS, HEAD_V), qbd, bias_s,
                            row_slope.reshape(LANES, 1), to_seq(ks32), to_seq(vs32), bias_new, lamv,
                            subln_g[0].reshape(1, HEAD_V), lam0, ds)
    o_s = o_s[:, :ds].transpose(1, 0, 2).reshape(ts, D_MODEL).astype(BF16)
    x1s, h2s, gTs = _oproj_call(o_s, xs, mod_s[1], wo[0], norm_ffn_g[1:2], wrT, rb)
    ys = _moe_call(h2s, gTs.T, x1s, mod_s[1][:, 5 * D_MODEL:], wgu[1], wd[1])

    seq_major = lambda a: a.reshape(ds, nb, D_MODEL).transpose(1, 0, 2)
    return (yp[None],
            seq_major(ys),
            pst[None, None, 1:],
            pool_s.transpose(1, 0, 2)[None],
            k32.reshape(1, 1, seq // page, page, N_HEADS, KEY_DIM),
            v32.reshape(1, 1, seq // page, page, N_HEADS, HEAD_V),
            seq_major(ks32).reshape(1, nb, ds, N_HEADS, KEY_DIM),
            seq_major(vs32).reshape(1, nb, ds, N_HEADS, HEAD_V))
```

```python
import functools
import math

import jax
import jax.numpy as jnp
from jax import lax
from jax.experimental import pallas as pl
from jax.experimental.pallas import tpu as pltpu

F32 = jnp.float32
BF16 = jnp.bfloat16
HIGHEST = lax.Precision.HIGHEST

D_MODEL = 1024
DEPTH = 2
POOL_WINDOWS = (2, 4, 8, 16)
POOL_GROUP = D_MODEL // len(POOL_WINDOWS)
POOL_BUF = max(POOL_WINDOWS) - 1
HALO = POOL_BUF + 1
N_HEADS = 8
HEAD_QK = 64
HEAD_V = 2 * HEAD_QK
KEY_DIM = 2 * HEAD_QK
N_EXPERTS = 16
EXPERTS_PER_GROUP = 4
N_GROUPS = N_EXPERTS // EXPERTS_PER_GROUP
D_EXPERT = D_MODEL // 2
RMS_EPS = 1e-6
LOG2E = 1.4426950408889634
LANES = 128
NEG_BIG = -1e30

ROW_TILE = 512
MOE_ROW_TILE = 1024
MOE_CHUNK = 160
ATTN_BQ = 1024
ATTN_BK = 256
ALIBI_PARTS = 3
DENOM_ROWS = 16
PAGES_PER_STEP = 8
VMEM_LIMIT = 56 * 1024 * 1024


def _cparams(*sem):
    return pltpu.CompilerParams(dimension_semantics=sem, vmem_limit_bytes=VMEM_LIMIT)


def _lambda_init(layer):
    return 0.8 - 0.6 * math.exp(-0.3 * layer)


def _truncate_to_bf16(x):
    bits = lax.bitcast_convert_type(x, jnp.uint32) & jnp.uint32(0xFFFF0000)
    return lax.bitcast_convert_type(bits, F32)


def _rms(x, g):
    ms = jnp.mean(x * x, axis=-1, keepdims=True)
    return x * lax.rsqrt(ms + RMS_EPS) * g


def _modulate(x, g, shift, scale):
    return _rms(x, g) * (1.0 + scale) + shift


def _split_mod(mod):
    return [mod[:, k * D_MODEL:(k + 1) * D_MODEL] for k in range(6)]


def _route_gates(h, wrT_ref, rb_ref, gT_ref):
    logits = lax.dot_general(wrT_ref[...], h, (((1,), (1,)), ((), ())),
                             precision=HIGHEST, preferred_element_type=F32)
    mx = jnp.max(logits, axis=0, keepdims=True)
    ex = jnp.exp(logits - mx)
    probs = ex / jnp.sum(ex, axis=0, keepdims=True)
    sel = probs + rb_ref[...]
    srow = [sel[e:e + 1] for e in range(N_EXPERTS)]
    prow = [probs[e:e + 1] for e in range(N_EXPERTS)]

    gscore = []
    for g in range(N_GROUPS):
        r = srow[g * EXPERTS_PER_GROUP:(g + 1) * EXPERTS_PER_GROUP]
        best = None
        for a in range(EXPERTS_PER_GROUP):
            for b in range(a + 1, EXPERTS_PER_GROUP):
                s = r[a] + r[b]
                best = s if best is None else jnp.maximum(best, s)
        gscore.append(best)
    gbest = functools.reduce(jnp.maximum, gscore)
    in_grp, taken = [], None
    for g in range(N_GROUPS):
        hit = gscore[g] == gbest
        if taken is not None:
            hit = jnp.logical_and(hit, jnp.logical_not(taken))
        taken = hit if taken is None else jnp.logical_or(taken, hit)
        in_grp.append(hit)

    def pick(rows, k):
        out = rows[(N_GROUPS - 1) * EXPERTS_PER_GROUP + k]
        for g in range(N_GROUPS - 2, -1, -1):
            out = jnp.where(in_grp[g], rows[g * EXPERTS_PER_GROUP + k], out)
        return out

    v = [pick(srow, k) for k in range(EXPERTS_PER_GROUP)]
    p = [pick(prow, k) for k in range(EXPERTS_PER_GROUP)]

    def first_hits(vals, target):
        hits, seen = [], None
        for x in vals:
            hit = x == target
            if seen is not None:
                hit = jnp.logical_and(hit, jnp.logical_not(seen))
            seen = hit if seen is None else jnp.logical_or(seen, hit)
            hits.append(hit)
        return hits

    top1 = first_hits(v, functools.reduce(jnp.maximum, v))
    rest = [jnp.where(top1[k], -jnp.inf, v[k]) for k in range(EXPERTS_PER_GROUP)]
    top2 = first_hits(rest, functools.reduce(jnp.maximum, rest))
    p1 = functools.reduce(jnp.add, [jnp.where(top1[k], p[k], 0.0) for k in range(EXPERTS_PER_GROUP)])
    p2 = functools.reduce(jnp.add, [jnp.where(top2[k], p[k], 0.0) for k in range(EXPERTS_PER_GROUP)])
    den = p1 + p2
    w1, w2 = p1 / den, p2 / den
    for g in range(N_GROUPS):
        for k in range(EXPERTS_PER_GROUP):
            val = jnp.where(top1[k], w1, jnp.where(top2[k], w2, 0.0))
            e = g * EXPERTS_PER_GROUP + k
            gT_ref[e:e + 1, :] = jnp.where(in_grp[g], val, 0.0)


def _ffn_prologue(x1, mod6, gffn_ref, wrT_ref, rb_ref, x1_ref, h2_ref, gT_ref):
    x1_ref[...] = x1
    h2 = _modulate(x1, gffn_ref[...], mod6[3], mod6[4])
    h2_ref[...] = h2.astype(BF16)
    _route_gates(h2, wrT_ref, rb_ref, gT_ref)


def _mod_kernel(c_ref, w_ref, b_ref, o_ref):
    c = c_ref[...]
    a = c * jax.nn.sigmoid(c)
    o_ref[0] = jnp.dot(a, w_ref[0], precision=HIGHEST, preferred_element_type=F32) + b_ref[0]


def _mod_call(c_all, w_mod, b_mod):
    rows = c_all.shape[0]
    tn = 1536
    return pl.pallas_call(
        _mod_kernel,
        grid=(DEPTH, 6 * D_MODEL // tn),
        in_specs=[pl.BlockSpec((rows, D_MODEL), lambda i, j: (0, 0)),
                  pl.BlockSpec((1, D_MODEL, tn), lambda i, j: (i, 0, j)),
                  pl.BlockSpec((1, 1, tn), lambda i, j: (i, 0, j))],
        out_specs=pl.BlockSpec((1, rows, tn), lambda i, j: (i, 0, j)),
        out_shape=jax.ShapeDtypeStruct((DEPTH, rows, 6 * D_MODEL), F32),
        compiler_params=_cparams("arbitrary", "arbitrary"),
        name="adaln_mod",
    )(c_all, w_mod, b_mod.reshape(DEPTH, 1, 6 * D_MODEL))


def _pool_prompt_kernel(x_ref, mod_ref, gmix_ref, gffn_ref, pw_ref, ps_ref, wrT_ref, rb_ref,
                        x1_ref, h2_ref, gT_ref, pst_ref, ext_ref, *, tm):
    i = pl.program_id(0)

    @pl.when(i == 0)
    def _():
        ext_ref[0:HALO, :] = jnp.zeros((HALO, D_MODEL), F32)

    x = x_ref[...]
    mod6 = _split_mod(mod_ref[...])
    ext_ref[HALO:HALO + tm, :] = _modulate(x, gmix_ref[...], mod6[0], mod6[1])
    pos = i * tm + lax.broadcasted_iota(jnp.int32, (tm, 1), 0)
    ys = []
    for g, w in enumerate(POOL_WINDOWS):
        c0 = g * POOL_GROUP
        cur = ext_ref[HALO:HALO + tm, c0:c0 + POOL_GROUP]
        acc = cur
        for k in range(1, w):
            acc = acc + ext_ref[HALO - k:HALO - k + tm, c0:c0 + POOL_GROUP]
        cnt = jnp.minimum(pos + 1, w).astype(F32)
        d = acc / cnt - cur
        ys.append(jnp.dot(d.astype(BF16), pw_ref[g], preferred_element_type=F32))
    y = jnp.concatenate(ys, axis=1) * ps_ref[...]
    tail = ext_ref[tm:tm + HALO, :]
    ext_ref[0:HALO, :] = tail
    pst_ref[...] = tail
    _ffn_prologue(x + mod6[2] * y, mod6, gffn_ref, wrT_ref, rb_ref, x1_ref, h2_ref, gT_ref)


def _pool_prompt_call(x, mod, gmix, gffn, pool_w, pool_scale, wrT, rb):
    t = x.shape[0]
    tm = ROW_TILE
    const = lambda shape: pl.BlockSpec(shape, lambda i: (0,) * len(shape))
    return pl.pallas_call(
        functools.partial(_pool_prompt_kernel, tm=tm),
        grid=(t // tm,),
        in_specs=[pl.BlockSpec((tm, D_MODEL), lambda i: (i, 0)),
                  const((1, 6 * D_MODEL)), const((1, D_MODEL)), const((1, D_MODEL)),
                  const(pool_w.shape), const((1, D_MODEL)),
                  const((N_EXPERTS, D_MODEL)), const((N_EXPERTS, 1))],
        out_specs=[pl.BlockSpec((tm, D_MODEL), lambda i: (i, 0)),
                   pl.BlockSpec((tm, D_MODEL), lambda i: (i, 0)),
                   pl.BlockSpec((N_EXPERTS, tm), lambda i: (0, i)),
                   const((HALO, D_MODEL))],
        out_shape=[jax.ShapeDtypeStruct((t, D_MODEL), F32),
                   jax.ShapeDtypeStruct((t, D_MODEL), BF16),
                   jax.ShapeDtypeStruct((N_EXPERTS, t), F32),
                   jax.ShapeDtypeStruct((HALO, D_MODEL), F32)],
        scratch_shapes=[pltpu.VMEM((tm + HALO, D_MODEL), F32)],
        compiler_params=_cparams("arbitrary"),
        name="pool_mixer_prompt",
    )(x, mod, gmix, gffn, pool_w, pool_scale, wrT, rb)


def _pool_sample_kernel(x_ref, st_ref, mod_ref, gmix_ref, gffn_ref, pw_ref, ps_ref, wrT_ref, rb_ref,
                        x1_ref, h2_ref, gT_ref, pool_ref, *, nb, ds):
    x = x_ref[...]
    mod6 = _split_mod(mod_ref[...])
    hs = _modulate(x, gmix_ref[...], mod6[0], mod6[1])
    slabs = [st_ref[r] for r in range(POOL_BUF)] + [hs[t * nb:(t + 1) * nb] for t in range(ds)]
    for r in range(POOL_BUF):
        pool_ref[r] = slabs[ds + r]
    ys = []
    for g, w in enumerate(POOL_WINDOWS):
        c0 = g * POOL_GROUP
        ds_rows = []
        for t in range(ds):
            p = POOL_BUF + t
            acc = slabs[p][:, c0:c0 + POOL_GROUP]
            for k in range(1, w):
                acc = acc + slabs[p - k][:, c0:c0 + POOL_GROUP]
            ds_rows.append(acc / float(w) - slabs[p][:, c0:c0 + POOL_GROUP])
        d = jnp.concatenate(ds_rows, axis=0)
        ys.append(jnp.dot(d, pw_ref[g], precision=HIGHEST, preferred_element_type=F32))
    y = jnp.concatenate(ys, axis=1) * ps_ref[...]
    _ffn_prologue(x + mod6[2] * y, mod6, gffn_ref, wrT_ref, rb_ref, x1_ref, h2_ref, gT_ref)


def _pool_sample_call(x, state, mod, gmix, gffn, pool_w, pool_scale, wrT, rb, nb, ds):
    t = x.shape[0]
    full = lambda a: pl.BlockSpec(a.shape, lambda: (0,) * a.ndim)
    args = (x, state, mod, gmix, gffn, pool_w, pool_scale, wrT, rb)
    out_shape = [jax.ShapeDtypeStruct((t, D_MODEL), F32),
                 jax.ShapeDtypeStruct((t, D_MODEL), BF16),
                 jax.ShapeDtypeStruct((N_EXPERTS, t), F32),
                 jax.ShapeDtypeStruct((POOL_BUF, nb, D_MODEL), F32)]
    return pl.pallas_call(
        functools.partial(_pool_sample_kernel, nb=nb, ds=ds),
        in_specs=[full(a) for a in args],
        out_specs=[full(s) for s in out_shape],
        out_shape=out_shape,
        compiler_params=pltpu.CompilerParams(vmem_limit_bytes=VMEM_LIMIT),
        name="pool_mixer_sample",
    )(*args)


def _moe_kernel(h_ref, gates_ref, x_ref, g2_ref, wgu_ref, wd_ref, o_ref, acc_ref):
    e = pl.program_id(1)

    @pl.when(e == 0)
    def _():
        acc_ref[...] = jnp.zeros_like(acc_ref)

    gu = jnp.dot(h_ref[...], wgu_ref[0], preferred_element_type=F32)
    gate_in = gu[:, :D_EXPERT]
    a = gate_in * jax.nn.sigmoid(gate_in) * gu[:, D_EXPERT:]
    gates = gates_ref[...]
    lane = lax.broadcasted_iota(jnp.int32, gates.shape, 1)
    gcol = jnp.sum(jnp.where(lane == e, gates, 0.0), axis=1, keepdims=True)
    acc_ref[...] += jnp.dot((a * gcol).astype(BF16), wd_ref[0], preferred_element_type=F32)

    @pl.when(e == N_EXPERTS - 1)
    def _():
        o_ref[...] = x_ref[...] + g2_ref[...] * acc_ref[...]


def _moe_call(h, gates, x, g2, wgu, wd):
    t = h.shape[0]
    tm = min(MOE_ROW_TILE, t)
    g2_rows = g2.shape[0]
    g2_spec = (pl.BlockSpec((1, D_MODEL), lambda i, e: (0, 0)) if g2_rows == 1
               else pl.BlockSpec((tm, D_MODEL), lambda i, e: (i, 0)))
    return pl.pallas_call(
        _moe_kernel,
        grid=(t // tm, N_EXPERTS),
        in_specs=[pl.BlockSpec((tm, D_MODEL), lambda i, e: (i, 0)),
                  pl.BlockSpec((tm, N_EXPERTS), lambda i, e: (i, 0)),
                  pl.BlockSpec((tm, D_MODEL), lambda i, e: (i, 0)),
                  g2_spec,
                  pl.BlockSpec((1, D_MODEL, 2 * D_EXPERT), lambda i, e: (e, 0, 0)),
                  pl.BlockSpec((1, D_EXPERT, D_MODEL), lambda i, e: (e, 0, 0))],
        out_specs=pl.BlockSpec((tm, D_MODEL), lambda i, e: (i, 0)),
        out_shape=jax.ShapeDtypeStruct((t, D_MODEL), F32),
        scratch_shapes=[pltpu.VMEM((tm, D_MODEL), F32)],
        compiler_params=_cparams("arbitrary", "arbitrary"),
        name="moe_dense",
    )(h, gates, x, g2, wgu, wd)


def _moe_routed_kernel(cnt_ref, h_ref, gates_ref, gatesT_ref, x_ref, g2_ref, before_ref, beforeT_ref,
                       wgu_ref, wd_ref, o_ref, acc_ref, rank_ref, rankT_ref, *, chunk):
    i = pl.program_id(0)
    e = pl.program_id(1)
    tm = h_ref.shape[0]

    @pl.when(e == 0)
    def _():
        acc_ref[...] = jnp.zeros_like(acc_ref)
        rankT_ref[...] = jnp.dot(jnp.where(gatesT_ref[...] > 0.0, 1.0, 0.0).astype(BF16), before_ref[...],
                                 preferred_element_type=F32)
        rank_ref[...] = jnp.dot(beforeT_ref[...], jnp.where(gates_ref[...] > 0.0, 1.0, 0.0).astype(BF16),
                                preferred_element_type=F32)

    g_row = gatesT_ref[pl.ds(e, 1), :]
    rank_row = rankT_ref[pl.ds(e, 1), :]
    pick = lax.broadcasted_iota(jnp.int32, (tm, N_EXPERTS), 1) == e
    g_col = jnp.sum(jnp.where(pick, gates_ref[...], 0.0), axis=1, keepdims=True)
    rank_col = jnp.sum(jnp.where(pick, rank_ref[...], 0.0), axis=1, keepdims=True)

    def body(c, carry):
        base = (c * chunk).astype(F32)
        slot_r = base + lax.broadcasted_iota(jnp.int32, (chunk, 1), 0).astype(F32)
        slot_c = base + lax.broadcasted_iota(jnp.int32, (1, chunk), 1).astype(F32)
        take = jnp.logical_and(rank_row == slot_r, g_row > 0.0)
        put = jnp.logical_and(rank_col == slot_c, g_col > 0.0)
        xg = jnp.dot(jnp.where(take, 1.0, 0.0).astype(BF16), h_ref[...],
                     preferred_element_type=F32).astype(BF16)
        gu = jnp.dot(xg, wgu_ref[0], preferred_element_type=F32)
        gate_in = gu[:, :D_EXPERT]
        a = gate_in * jax.nn.sigmoid(gate_in) * gu[:, D_EXPERT:]
        g_sel = jnp.sum(jnp.where(take, g_row, 0.0), axis=1, keepdims=True)
        y = jnp.dot((a * g_sel).astype(BF16), wd_ref[0], preferred_element_type=F32)
        acc_ref[...] += jnp.dot(jnp.where(put, 1.0, 0.0).astype(BF16), y.astype(BF16),
                                preferred_element_type=F32)
        return carry

    lax.fori_loop(0, (cnt_ref[e, i] + chunk - 1) // chunk, body, 0)

    @pl.when(e == N_EXPERTS - 1)
    def _():
        o_ref[...] = x_ref[...] + g2_ref[...] * acc_ref[...]


def _moe_routed_call(h, gatesT, x, g2, wgu, wd):
    t = h.shape[0]
    tm = MOE_ROW_TILE
    n_tiles = t // tm
    gates = gatesT.T
    counts = jnp.sum((gatesT > 0.0).reshape(N_EXPERTS, n_tiles, tm), axis=-1).astype(jnp.int32)
    before = (jnp.arange(tm)[:, None] < jnp.arange(tm)[None, :]).astype(BF16)
    grid_spec = pltpu.PrefetchScalarGridSpec(
        num_scalar_prefetch=1,
        grid=(n_tiles, N_EXPERTS),
        in_specs=[pl.BlockSpec((tm, D_MODEL), lambda i, e, cnt: (i, 0)),
                  pl.BlockSpec((tm, N_EXPERTS), lambda i, e, cnt: (i, 0)),
                  pl.BlockSpec((N_EXPERTS, tm), lambda i, e, cnt: (0, i)),
                  pl.BlockSpec((tm, D_MODEL), lambda i, e, cnt: (i, 0)),
                  pl.BlockSpec((1, D_MODEL), lambda i, e, cnt: (0, 0)),
                  pl.BlockSpec((tm, tm), lambda i, e, cnt: (0, 0)),
                  pl.BlockSpec((tm, tm), lambda i, e, cnt: (0, 0)),
                  pl.BlockSpec((1, D_MODEL, 2 * D_EXPERT), lambda i, e, cnt: (e, 0, 0)),
                  pl.BlockSpec((1, D_EXPERT, D_MODEL), lambda i, e, cnt: (e, 0, 0))],
        out_specs=pl.BlockSpec((tm, D_MODEL), lambda i, e, cnt: (i, 0)),
        scratch_shapes=[pltpu.VMEM((tm, D_MODEL), F32), pltpu.VMEM((tm, N_EXPERTS), F32),
                        pltpu.VMEM((N_EXPERTS, tm), F32)])
    return pl.pallas_call(
        functools.partial(_moe_routed_kernel, chunk=MOE_CHUNK),
        grid_spec=grid_spec,
        out_shape=jax.ShapeDtypeStruct((t, D_MODEL), F32),
        compiler_params=_cparams("arbitrary", "arbitrary"),
        name="moe_routed",
    )(counts, h, gates, gatesT, x, g2, before, before.T, wgu, wd)


def _seg_norm(t, g_full, seg_ref, segT_ref):
    sq = t * t
    hi = sq.astype(BF16)
    lo = (sq - hi.astype(F32)).astype(BF16)
    ssq = (jnp.dot(hi, seg_ref[...], preferred_element_type=F32)
           + jnp.dot(lo, seg_ref[...], preferred_element_type=F32))
    inv = lax.rsqrt(ssq * (1.0 / HEAD_QK) + RMS_EPS)
    ihi = inv.astype(BF16)
    ilo = (inv - ihi.astype(F32)).astype(BF16)
    inv_full = (jnp.dot(ihi, segT_ref[...], preferred_element_type=F32)
                + jnp.dot(ilo, segT_ref[...], preferred_element_type=F32))
    return t * inv_full * g_full


def _qkv_kernel(x_ref, mod_ref, gmix_ref, w_ref, gq_ref, gk_ref, seg_ref, segT_ref, *out_refs,
                prompt):
    mod6 = _split_mod(mod_ref[...])
    h = _modulate(x_ref[...], gmix_ref[...], mod6[0], mod6[1]).astype(BF16)
    qkv = jnp.dot(h, w_ref[...], preferred_element_type=F32)
    q = _seg_norm(qkv[:, :D_MODEL], gq_ref[...], seg_ref, segT_ref)
    k = _seg_norm(qkv[:, D_MODEL:2 * D_MODEL], gk_ref[...], seg_ref, segT_ref)
    v = qkv[:, 2 * D_MODEL:]
    q_ref, k32_ref, v32_ref = out_refs[:3]
    q_ref[...] = (q * (HEAD_QK ** -0.5 * LOG2E)).astype(BF16)
    k32_ref[...] = k
    v32_ref[...] = v
    if prompt:
        kb_ref, vT_ref = out_refs[3:]
        kb_ref[...] = k.astype(BF16)
        vT = v.T.astype(BF16)
        for b in range(vT_ref.shape[0]):
            vT_ref[b] = vT[:, b * ATTN_BK:(b + 1) * ATTN_BK]


def _qkv_call(x, mod, gmix, wqkv, gq, gk, seg, segT, prompt):
    t = x.shape[0]
    tm = min(ROW_TILE, t)
    const = lambda shape: pl.BlockSpec(shape, lambda i: (0,) * len(shape))
    row = lambda: pl.BlockSpec((tm, D_MODEL), lambda i: (i, 0))
    mod_spec = const((1, 6 * D_MODEL)) if mod.shape[0] == 1 else pl.BlockSpec((tm, 6 * D_MODEL), lambda i: (i, 0))
    out_specs = [row(), row(), row()]
    out_shape = [jax.ShapeDtypeStruct((t, D_MODEL), BF16),
                 jax.ShapeDtypeStruct((t, D_MODEL), F32),
                 jax.ShapeDtypeStruct((t, D_MODEL), F32)]
    if prompt:
        out_specs += [row(), pl.BlockSpec((tm // ATTN_BK, D_MODEL, ATTN_BK), lambda i: (i, 0, 0))]
        out_shape += [jax.ShapeDtypeStruct((t, D_MODEL), BF16),
                      jax.ShapeDtypeStruct((t // ATTN_BK, D_MODEL, ATTN_BK), BF16)]
    return pl.pallas_call(
        functools.partial(_qkv_kernel, prompt=prompt),
        grid=(t // tm,),
        in_specs=[row(), mod_spec, const((1, D_MODEL)), const(wqkv.shape),
                  const((1, D_MODEL)), const((1, D_MODEL)), const(seg.shape), const(segT.shape)],
        out_specs=out_specs,
        out_shape=out_shape,
        compiler_params=_cparams("arbitrary"),
        name="qkv_prompt" if prompt else "qkv_sample",
    )(x, mod, gmix, wqkv, gq, gk, seg, segT)


def _diff_lambda(lamv, lam0):
    a = jnp.sum(lamv[0:1] * lamv[1:2], axis=1, keepdims=True)
    b = jnp.sum(lamv[2:3] * lamv[3:4], axis=1, keepdims=True)
    return jnp.exp(a) - jnp.exp(b) + lam0


def _attn_prompt_kernel(slope_ref, q_ref, k_ref, vT_ref, kbias_ref, lamv_ref, sg_ref, o_ref,
                        sa_ref, sb_ref, pa_ref, pb_ref, m_ref, alpha_ref, acc_ref, *, bq, bk, lam0):
    hd = pl.program_id(0)
    qi = pl.program_id(1)
    slope = slope_ref[hd]
    q0 = qi * bq
    per_q = bq // bk

    qT = q_ref[...].astype(F32).T
    row = lax.broadcasted_iota(jnp.int32, qT.shape, 0)
    row2 =lax.broadcasted_iota(jnp.int32, (KEY_DIM, 2 * bq), 0)
    qp = jnp.concatenate(
        [jnp.concatenate([jnp.where(row < HEAD_QK, qT, 0.0), jnp.where(row >= HEAD_QK, qT, 0.0)], axis=1),
         jnp.where(row2 < ALIBI_PARTS, 1.0, 0.0)], axis=0).astype(BF16)
    kbias = kbias_ref[0]
    ones_rows = jnp.ones((DENOM_ROWS, bk), BF16)

    s_bufs, p_bufs = (sa_ref, sb_ref), (pa_ref, pb_ref)

    def shift_of(j):
        return slope * (j * bk - q0).astype(F32)

    def step(i, par, diag, do_scores=True, do_exp=True):
        v_aug = jnp.concatenate([vT_ref[jnp.maximum(i - 2, 0)], ones_rows], axis=0)
        acc_ref[...] = alpha_ref[par] * acc_ref[...] + jnp.dot(v_aug, p_bufs[par][...],
                                                              preferred_element_type=F32)
        m_prev = m_ref[...]
        if do_exp:
            p_bufs[1 - par][...] = jnp.exp2(s_bufs[1 - par][...] - (m_prev - shift_of(i - 1))).astype(BF16)
        if do_scores:
            kblk = k_ref[pl.ds(pl.multiple_of(i * bk, bk), bk), :]
            s = jnp.dot(jnp.concatenate([kblk, kbias], axis=1), qp, preferred_element_type=F32)
            if diag is not None:
                kk = lax.broadcasted_iota(jnp.int32, s.shape, 0) + diag
                qq = lax.broadcasted_iota(jnp.int32, s.shape, 1)
                s = jnp.where(kk <= jnp.where(qq >= bq, qq - bq, qq), s, -jnp.inf)
            s_bufs[par][...] = s
            m_new = jnp.maximum(m_prev, jnp.max(s, axis=0, keepdims=True) + shift_of(i))
            alpha_ref[par] = jnp.exp2(m_prev - m_new)
            m_ref[...] = m_new

    m_ref[...] = jnp.full(m_ref.shape, NEG_BIG, F32)
    acc_ref[...] = jnp.zeros_like(acc_ref)
    alpha_ref[...] = jnp.ones_like(alpha_ref)
    pa_ref[...] = jnp.zeros_like(pa_ref)
    sb_ref[...] = jnp.full(sb_ref.shape, -jnp.inf, F32)
    n_full = qi * per_q

    def pair(t, carry):
        step(2 * t, 0, None)
        step(2 * t + 1, 1, None)
        return carry

    lax.fori_loop(0, n_full // 2, pair, 0)
    for d in range(per_q + 2):
        step(n_full + d, d % 2, d * bk, do_scores=d < per_q, do_exp=d <= per_q)

    acc = acc_ref[...]
    o_all = acc[:HEAD_V] / acc[HEAD_V:HEAD_V + 1]
    lam = _diff_lambda(lamv_ref[...], lam0)
    o = o_all[:, :bq] - lam * o_all[:, bq:]
    ms = jnp.mean(o * o, axis=0, keepdims=True)
    o = o * lax.rsqrt(ms + RMS_EPS) * sg_ref[...] * (1.0 - lam0)
    o_ref[...] = o.T.astype(BF16)


def _attn_prompt_call(slopes_l2, q, kb, vT, kbias, lamv, sg, lam0):
    t = q.shape[0]
    bq, bk = ATTN_BQ, ATTN_BK
    assert bq % (2 * bk) == 0
    nk = t // bk
    acc_rows = HEAD_V + DENOM_ROWS
    return pl.pallas_call(
        functools.partial(_attn_prompt_kernel, bq=bq, bk=bk, lam0=lam0),
        grid=(N_HEADS, t // bq),
        in_specs=[pl.BlockSpec(memory_space=pltpu.SMEM),
                  pl.BlockSpec((bq, KEY_DIM), lambda h, i: (i, h)),
                  pl.BlockSpec((t, KEY_DIM), lambda h, i: (0, h)),
                  pl.BlockSpec((nk, HEAD_V, bk), lambda h, i: (0, h, 0)),
                  pl.BlockSpec((1, bk, KEY_DIM), lambda h, i: (h, 0, 0)),
                  pl.BlockSpec(lamv.shape, lambda h, i: (0, 0)),
                  pl.BlockSpec(sg.shape, lambda h, i: (0, 0))],
        out_specs=pl.BlockSpec((bq, HEAD_V), lambda h, i: (i, h)),
        out_shape=jax.ShapeDtypeStruct((t, D_MODEL), BF16),
        scratch_shapes=[pltpu.VMEM((bk, 2 * bq), F32), pltpu.VMEM((bk, 2 * bq), F32),
                        pltpu.VMEM((bk, 2 * bq), BF16), pltpu.VMEM((bk, 2 * bq), BF16),
                        pltpu.VMEM((1, 2 * bq), F32), pltpu.VMEM((2, 1, 2 * bq), F32),
                        pltpu.VMEM((acc_rows, 2 * bq), F32)],
        compiler_params=_cparams("arbitrary", "arbitrary"),
        name="diff_attn_prompt",
    )(slopes_l2, q, kb, vT, kbias, lamv, sg)


def _attn_sample_kernel(pt_ref, *refs, n_pages, page, ds, lam0):
    k_refs = refs[:n_pages]
    v_refs = refs[n_pages:2 * n_pages]
    (qbd_ref, bias_ref, slope_ref, knew_ref, vnew_ref, bnew_ref, lamv_ref, sg_ref,
     o_ref, m_ref, l_ref, acc_ref) = refs[2 * n_pages:]
    c = pl.program_id(1)
    nc = pl.num_programs(1)
    chunk = n_pages * page

    @pl.when(c == 0)
    def _():
        m_ref[...] = jnp.full(m_ref.shape, NEG_BIG, F32)
        l_ref[...] = jnp.zeros_like(l_ref)
        acc_ref[...] = jnp.zeros_like(acc_ref)

    def update(s, shift, v_bf16):
        m_old = m_ref[...]
        m_new = jnp.maximum(m_old, jnp.max(s, axis=1, keepdims=True) + shift)
        alpha = jnp.exp2(m_old - m_new)
        p = jnp.exp2(s - (m_new - shift))
        l_ref[...] = alpha * l_ref[...] + jnp.sum(p, axis=1, keepdims=True)
        acc_ref[...] = alpha * acc_ref[...] + jnp.dot(p.astype(BF16), v_bf16,
                                                      preferred_element_type=F32)
        m_ref[...] = m_new

    def head_major(refs):
        return jnp.concatenate(
            [jnp.concatenate([r[pl.ds(hd, page, stride=N_HEADS), :].astype(BF16) for r in refs], axis=0)
             for hd in range(N_HEADS)], axis=1)

    sT = jnp.dot(head_major(k_refs), qbd_ref[0], preferred_element_type=F32)
    s = sT.T + bias_ref[...]
    past = nc * chunk
    shift = slope_ref[...] * (c * chunk - past).astype(F32)
    update(s, shift, head_major(v_refs))

    @pl.when(c == nc - 1)
    def _():
        sTn = jnp.dot(knew_ref[0], qbd_ref[0], preferred_element_type=F32)
        sn = sTn.T + bnew_ref[...]
        update(sn, jnp.zeros_like(slope_ref[...]), vnew_ref[0])

        lam = _diff_lambda(lamv_ref[...], lam0)
        acc = acc_ref[...]
        l = l_ref[...]
        rows = 2 * ds
        for hd in range(N_HEADS):
            blk = acc[hd * rows:(hd + 1) * rows, hd * HEAD_V:(hd + 1) * HEAD_V] / l[hd * rows:(hd + 1) * rows]
            o = blk[0:ds] - lam * blk[ds:rows]
            ms = jnp.mean(o * o, axis=1, keepdims=True)
            o = o * lax.rsqrt(ms + RMS_EPS) * sg_ref[...] * (1.0 - lam0)
            o_ref[0, 0:ds, hd * HEAD_V:(hd + 1) * HEAD_V] = o
            o_ref[0, ds:rows, hd * HEAD_V:(hd + 1) * HEAD_V] = jnp.zeros((rows - ds, HEAD_V), F32)


def _attn_sample_call(page_table, cache_k, cache_v, qbd, bias, slope_col, knew, vnew, bnew, lamv, sg_row,
                      lam0, ds):
    nb, n_pages_total = page_table.shape
    page = cache_k.shape[1] // N_HEADS
    n_pages = PAGES_PER_STEP
    chunk = n_pages * page
    rows = 2 * ds
    page_spec = lambda i: pl.BlockSpec((None, page * N_HEADS, KEY_DIM),
                                       lambda b, c, pt, i=i: (pt[b, c * n_pages + i], 0, 0))
    const2 = lambda a: pl.BlockSpec(a.shape, lambda b, c, pt: (0, 0))
    per_seq = lambda a: pl.BlockSpec((1,) + a.shape[1:], lambda b, c, pt: (b, 0, 0))
    grid_spec = pltpu.PrefetchScalarGridSpec(
        num_scalar_prefetch=1,
        grid=(nb, n_pages_total // n_pages),
        in_specs=([page_spec(i) for i in range(n_pages)] + [page_spec(i) for i in range(n_pages)]
                  + [per_seq(qbd), const2(bias), const2(slope_col), per_seq(knew), per_seq(vnew),
                     const2(bnew), const2(lamv), const2(sg_row)]),
        out_specs=pl.BlockSpec((1, rows, D_MODEL), lambda b, c, pt: (b, 0, 0)),
        scratch_shapes=[pltpu.VMEM((LANES, 1), F32), pltpu.VMEM((LANES, 1), F32),
                        pltpu.VMEM((LANES, D_MODEL), F32)])
    return pl.pallas_call(
        functools.partial(_attn_sample_kernel, n_pages=n_pages, page=page, ds=ds, lam0=lam0),
        grid_spec=grid_spec,
        out_shape=jax.ShapeDtypeStruct((nb, rows, D_MODEL), F32),
        compiler_params=_cparams("arbitrary", "arbitrary"),
        name="diff_attn_sample",
    )(page_table, *([cache_k] * n_pages), *([cache_v] * n_pages), qbd, bias, slope_col, knew, vnew,
      bnew, lamv, sg_row)


def _oproj_kernel(o_ref, x_ref, mod_ref, wo_ref, gffn_ref, wrT_ref, rb_ref, x1_ref, h2_ref, gT_ref):
    mod6 = _split_mod(mod_ref[...])
    y = jnp.dot(o_ref[...], wo_ref[...], preferred_element_type=F32)
    _ffn_prologue(x_ref[...] + mod6[2] * y, mod6, gffn_ref, wrT_ref, rb_ref, x1_ref, h2_ref, gT_ref)


def _oproj_call(o, x, mod, wo, gffn, wrT, rb):
    t = x.shape[0]
    tm = min(ROW_TILE, t)
    const = lambda shape: pl.BlockSpec(shape, lambda i: (0,) * len(shape))
    row = lambda: pl.BlockSpec((tm, D_MODEL), lambda i: (i, 0))
    mod_spec = const((1, 6 * D_MODEL)) if mod.shape[0] == 1 else pl.BlockSpec((tm, 6 * D_MODEL), lambda i: (i, 0))
    return pl.pallas_call(
        _oproj_kernel,
        grid=(t // tm,),
        in_specs=[row(), row(), mod_spec, const(wo.shape), const((1, D_MODEL)),
                  const((N_EXPERTS, D_MODEL)), const((N_EXPERTS, 1))],
        out_specs=[row(), row(), pl.BlockSpec((N_EXPERTS, tm), lambda i: (0, i))],
        out_shape=[jax.ShapeDtypeStruct((t, D_MODEL), F32),
                   jax.ShapeDtypeStruct((t, D_MODEL), BF16),
                   jax.ShapeDtypeStruct((N_EXPERTS, t), F32)],
        compiler_params=_cparams("arbitrary"),
        name="attn_out_proj",
    )(o, x, mod, wo, gffn, wrT, rb)


def kernel(x_prompt, x_sample, state_pool, cache_k, cache_v, page_table, c_prompt, c_sample,
           w_mod, b_mod, norm_mix_g, norm_ffn_g, pool_w, pool_scale, w_qkv, q_norm_g, k_norm_g,
           lambda_q1, lambda_k1, lambda_q2, lambda_k2, subln_g, w_o, w_router, router_bias,
           w_gate, w_up, w_down):
    assert DEPTH == 2 and x_prompt.shape[0] == 1
    seq = x_prompt.shape[1]
    nb, ds, _ = x_sample.shape
    n_phys, page = cache_k.shape[1], cache_k.shape[2]
    past = page_table.shape[1] * page
    ts = nb * ds

    c_all = jnp.concatenate([c_prompt, jnp.zeros((7, D_MODEL), F32), c_sample], axis=0)
    mods = _mod_call(c_all, w_mod, b_mod)
    mod_p = mods[:, 0:1]
    mod_s = jnp.tile(mods[:, 8:8 + nb], (1, ds, 1))

    pool_w_b = pool_w.astype(BF16)
    wgu = jnp.concatenate([w_gate, w_up], axis=-1).astype(BF16)
    wd = w_down.astype(BF16)
    wqkv = w_qkv.astype(BF16)
    wo = w_o.astype(BF16)
    wrT = w_router.T
    rb = router_bias.reshape(N_EXPERTS, 1)
    gq = jnp.tile(q_norm_g.reshape(1, 1, KEY_DIM), (1, N_HEADS, 1)).reshape(-1, 1, D_MODEL)
    gk = jnp.tile(k_norm_g.reshape(1, 1, KEY_DIM), (1, N_HEADS, 1)).reshape(-1, 1, D_MODEL)
    seg_id = jnp.arange(D_MODEL) // HEAD_QK
    seg = (seg_id[:, None] == jnp.arange(LANES)[None, :]).astype(BF16)
    segT = seg.T
    slopes = 2.0 ** (-8.0 * jnp.arange(1, N_HEADS + 1, dtype=F32) / N_HEADS)
    slopes_l2 = slopes * LOG2E

    xs = x_sample.transpose(1, 0, 2).reshape(ts, D_MODEL)
    state = state_pool.transpose(0, 2, 1, 3)

    x1p, h2p, gTp, pst = _pool_prompt_call(x_prompt[0], mod_p[0], norm_mix_g[0:1], norm_ffn_g[0:1],
                                           pool_w_b[0], pool_scale[0:1], wrT, rb)
    x1s, h2s, gTs, pool_s = _pool_sample_call(xs, state[0], mod_s[0], norm_mix_g[0:1], norm_ffn_g[0:1],
                                              pool_w[0], pool_scale[0:1], wrT, rb, nb, ds)
    xp = _moe_routed_call(h2p, gTp, x1p, mod_p[0][:, 5 * D_MODEL:], wgu[0], wd[0])
    xs = _moe_call(h2s, gTs.T, x1s, mod_s[0][:, 5 * D_MODEL:], wgu[0], wd[0])

    lam0 = _lambda_init(1)
    lamv = jnp.stack([lambda_q1[0], lambda_k1[0], lambda_q2[0], lambda_k2[0]])
    qb, k32, v32, kb, vT = _qkv_call(xp, mod_p[1], norm_mix_g[1:2], wqkv[0], gq[0], gk[0], seg, segT, True)
    rest = slopes_l2[:, None] * jnp.arange(ATTN_BK, dtype=F32)[None, :]
    parts = []
    for _ in range(ALIBI_PARTS):
        part = _truncate_to_bf16(rest)
        parts.append(part)
        rest = rest - part
    kbias = jnp.pad(jnp.stack(parts, axis=-1).astype(BF16), ((0, 0), (0, 0), (0, KEY_DIM - ALIBI_PARTS)))
    o_p = _attn_prompt_call(slopes_l2, qb, kb, vT, kbias, lamv, subln_g[0].reshape(HEAD_V, 1), lam0)
    x1p, h2p, gTp = _oproj_call(o_p, xp, mod_p[1], wo[0], norm_ffn_g[1:2], wrT, rb)
    yp = _moe_routed_call(h2p, gTp, x1p, mod_p[1][:, 5 * D_MODEL:], wgu[1], wd[1])

    qs, ks32, vs32 = _qkv_call(xs, mod_s[1], norm_mix_g[1:2], wqkv[0], gq[0], gk[0], seg, segT, False)
    rows = 2 * ds
    q_seq = qs.astype(F32).reshape(ds, nb, D_MODEL).transpose(1, 2, 0)
    colmask = (jnp.arange(D_MODEL)[:, None] // HEAD_QK) == (jnp.arange(N_HEADS * rows)[None, :] // ds)
    qbd = jnp.where(colmask[None], jnp.tile(q_seq, (1, 1, 2 * N_HEADS)), 0.0)
    qbd = jnp.pad(qbd, ((0, 0), (0, 0), (0, LANES - N_HEADS * rows))).astype(BF16)
    row_slope = jnp.pad(jnp.repeat(slopes_l2, rows), (0, LANES - N_HEADS * rows))
    chunk = PAGES_PER_STEP * page
    bias_s = row_slope[:, None] * jnp.arange(chunk, dtype=F32)[None, :]
    jn = jnp.arange(LANES)[None, :]
    tok = (jnp.arange(LANES) % ds)[:, None]
    bias_new = jnp.where(jnp.logical_and(jn <= tok, jn < ds), row_slope[:, None] * jn.astype(F32), -jnp.inf)
    to_seq = lambda a: jnp.pad(a.reshape(ds, nb, D_MODEL).transpose(1, 0, 2),
                               ((0, 0), (0, LANES - ds), (0, 0))).astype(BF16)
    o_s = _attn_sample_call(page_table, cache_k.reshape(-1, page * N_HEADS, KEY_DIM),
                            cache_v.reshape(-1, page * N_HEADS, HEAD_V), qbd, bias_s,
                            row_slope.reshape(LANES, 1), to_seq(ks32), to_seq(vs32), bias_new, lamv,
                            subln_g[0].reshape(1, HEAD_V), lam0, ds)
    o_s = o_s[:, :ds].transpose(1, 0, 2).reshape(ts, D_MODEL).astype(BF16)
    x1s, h2s, gTs = _oproj_call(o_s, xs, mod_s[1], wo[0], norm_ffn_g[1:2], wrT, rb)
    ys = _moe_call(h2s, gTs.T, x1s, mod_s[1][:, 5 * D_MODEL:], wgu[1], wd[1])

    seq_major = lambda a: a.reshape(ds, nb, D_MODEL).transpose(1, 0, 2)
    return (yp[None],
            seq_major(ys),
            pst[None, None, 1:],
            pool_s.transpose(1, 0, 2)[None],
            k32.reshape(1, 1, seq // page, page, N_HEADS, KEY_DIM),
            v32.reshape(1, 1, seq // page, page, N_HEADS, HEAD_V),
            seq_major(ks32).reshape(1, nb, ds, N_HEADS, KEY_DIM),
            seq_major(vs32).reshape(1, nb, ds, N_HEADS, HEAD_V))
```

```python
import functools
import math

import jax
import jax.numpy as jnp
from jax import lax
from jax.experimental import pallas as pl
from jax.experimental.pallas import tpu as pltpu

F32 = jnp.float32
BF16 = jnp.bfloat16
HIGHEST = lax.Precision.HIGHEST

D_MODEL = 1024
DEPTH = 2
POOL_WINDOWS = (2, 4, 8, 16)
POOL_GROUP = D_MODEL // len(POOL_WINDOWS)
POOL_BUF = max(POOL_WINDOWS) - 1
HALO = POOL_BUF + 1
N_HEADS = 8
HEAD_QK = 64
HEAD_V = 2 * HEAD_QK
KEY_DIM = 2 * HEAD_QK
N_EXPERTS = 16
EXPERTS_PER_GROUP = 4
N_GROUPS = N_EXPERTS // EXPERTS_PER_GROUP
D_EXPERT = D_MODEL // 2
RMS_EPS = 1e-6
LOG2E = 1.4426950408889634
LANES = 128
NEG_BIG = -1e30

ROW_TILE = 512
MOE_ROW_TILE = 1024
MOE_CHUNK = 256
MOE_TAIL_CHUNKS = (128,)
ATTN_BQ = 1024
ATTN_BK = 256
ALIBI_PARTS = 3
DENOM_ROWS = 16
PAGES_PER_STEP = 16
VMEM_LIMIT = 56 * 1024 * 1024


def _cparams(*sem):
    return pltpu.CompilerParams(dimension_semantics=sem, vmem_limit_bytes=VMEM_LIMIT)


def _lambda_init(layer):
    return 0.8 - 0.6 * math.exp(-0.3 * layer)


def _truncate_to_bf16(x):
    bits = lax.bitcast_convert_type(x, jnp.uint32) & jnp.uint32(0xFFFF0000)
    return lax.bitcast_convert_type(bits, F32)


def _rms(x, g):
    ms = jnp.mean(x * x, axis=-1, keepdims=True)
    return x * lax.rsqrt(ms + RMS_EPS) * g


def _modulate(x, g, shift, scale):
    return _rms(x, g) * (1.0 + scale) + shift


def _split_mod(mod):
    return [mod[:, k * D_MODEL:(k + 1) * D_MODEL] for k in range(6)]


def _route_gates(h, wrT_ref, rb_ref, gT_ref):
    logits = lax.dot_general(wrT_ref[...], h, (((1,), (1,)), ((), ())),
                             precision=HIGHEST, preferred_element_type=F32)
    mx = jnp.max(logits, axis=0, keepdims=True)
    ex = jnp.exp(logits - mx)
    probs = ex / jnp.sum(ex, axis=0, keepdims=True)
    sel = probs + rb_ref[...]
    srow = [sel[e:e + 1] for e in range(N_EXPERTS)]
    prow = [probs[e:e + 1] for e in range(N_EXPERTS)]

    gscore = []
    for g in range(N_GROUPS):
        r = srow[g * EXPERTS_PER_GROUP:(g + 1) * EXPERTS_PER_GROUP]
        best = None
        for a in range(EXPERTS_PER_GROUP):
            for b in range(a + 1, EXPERTS_PER_GROUP):
                s = r[a] + r[b]
                best = s if best is None else jnp.maximum(best, s)
        gscore.append(best)
    gbest = functools.reduce(jnp.maximum, gscore)
    in_grp, taken = [], None
    for g in range(N_GROUPS):
        hit = gscore[g] == gbest
        if taken is not None:
            hit = jnp.logical_and(hit, jnp.logical_not(taken))
        taken = hit if taken is None else jnp.logical_or(taken, hit)
        in_grp.append(hit)

    def pick(rows, k):
        out = rows[(N_GROUPS - 1) * EXPERTS_PER_GROUP + k]
        for g in range(N_GROUPS - 2, -1, -1):
            out = jnp.where(in_grp[g], rows[g * EXPERTS_PER_GROUP + k], out)
        return out

    v = [pick(srow, k) for k in range(EXPERTS_PER_GROUP)]
    p = [pick(prow, k) for k in range(EXPERTS_PER_GROUP)]

    def first_hits(vals, target):
        hits, seen = [], None
        for x in vals:
            hit = x == target
            if seen is not None:
                hit = jnp.logical_and(hit, jnp.logical_not(seen))
            seen = hit if seen is None else jnp.logical_or(seen, hit)
            hits.append(hit)
        return hits

    top1 = first_hits(v, functools.reduce(jnp.maximum, v))
    rest = [jnp.where(top1[k], -jnp.inf, v[k]) for k in range(EXPERTS_PER_GROUP)]
    top2 = first_hits(rest, functools.reduce(jnp.maximum, rest))
    p1 = functools.reduce(jnp.add, [jnp.where(top1[k], p[k], 0.0) for k in range(EXPERTS_PER_GROUP)])
    p2 = functools.reduce(jnp.add, [jnp.where(top2[k], p[k], 0.0) for k in range(EXPERTS_PER_GROUP)])
    den = p1 + p2
    w1, w2 = p1 / den, p2 / den
    for g in range(N_GROUPS):
        for k in range(EXPERTS_PER_GROUP):
            val = jnp.where(top1[k], w1, jnp.where(top2[k], w2, 0.0))
            e = g * EXPERTS_PER_GROUP + k
            gT_ref[e:e + 1, :] = jnp.where(in_grp[g], val, 0.0)


def _ffn_prologue(x1, mod6, gffn_ref, wrT_ref, rb_ref, x1_ref, h2_ref, gT_ref):
    x1_ref[...] = x1
    h2 = _modulate(x1, gffn_ref[...], mod6[3], mod6[4])
    h2_ref[...] = h2.astype(BF16)
    _route_gates(h2, wrT_ref, rb_ref, gT_ref)


def _mod_kernel(c_ref, w_ref, b_ref, o_ref):
    c = c_ref[...]
    a = c * jax.nn.sigmoid(c)
    o_ref[0] = jnp.dot(a, w_ref[0], precision=HIGHEST, preferred_element_type=F32) + b_ref[0]


def _mod_call(c_all, w_mod, b_mod):
    rows = c_all.shape[0]
    tn = 1536
    return pl.pallas_call(
        _mod_kernel,
        grid=(DEPTH, 6 * D_MODEL // tn),
        in_specs=[pl.BlockSpec((rows, D_MODEL), lambda i, j: (0, 0)),
                  pl.BlockSpec((1, D_MODEL, tn), lambda i, j: (i, 0, j)),
                  pl.BlockSpec((1, 1, tn), lambda i, j: (i, 0, j))],
        out_specs=pl.BlockSpec((1, rows, tn), lambda i, j: (i, 0, j)),
        out_shape=jax.ShapeDtypeStruct((DEPTH, rows, 6 * D_MODEL), F32),
        compiler_params=_cparams("arbitrary", "arbitrary"),
        name="adaln_mod",
    )(c_all, w_mod, b_mod.reshape(DEPTH, 1, 6 * D_MODEL))


def _pool_prompt_kernel(x_ref, mod_ref, gmix_ref, gffn_ref, pw_ref, ps_ref, wrT_ref, rb_ref,
                        x1_ref, h2_ref, gT_ref, pst_ref, ext_ref, *, tm):
    i = pl.program_id(0)

    @pl.when(i == 0)
    def _():
        ext_ref[0:HALO, :] = jnp.zeros((HALO, D_MODEL), F32)

    x = x_ref[...]
    mod6 = _split_mod(mod_ref[...])
    ext_ref[HALO:HALO + tm, :] = _modulate(x, gmix_ref[...], mod6[0], mod6[1])
    pos = i * tm + lax.broadcasted_iota(jnp.int32, (tm, 1), 0)
    ys = []
    for g, w in enumerate(POOL_WINDOWS):
        c0 = g * POOL_GROUP
        cur = ext_ref[HALO:HALO + tm, c0:c0 + POOL_GROUP]
        acc = cur
        for k in range(1, w):
            acc = acc + ext_ref[HALO - k:HALO - k + tm, c0:c0 + POOL_GROUP]
        cnt = jnp.minimum(pos + 1, w).astype(F32)
        d = acc / cnt - cur
        ys.append(jnp.dot(d.astype(BF16), pw_ref[g], preferred_element_type=F32))
    y = jnp.concatenate(ys, axis=1) * ps_ref[...]
    tail = ext_ref[tm:tm + HALO, :]
    ext_ref[0:HALO, :] = tail
    pst_ref[...] = tail
    _ffn_prologue(x + mod6[2] * y, mod6, gffn_ref, wrT_ref, rb_ref, x1_ref, h2_ref, gT_ref)


def _pool_prompt_call(x, mod, gmix, gffn, pool_w, pool_scale, wrT, rb):
    t = x.shape[0]
    tm = ROW_TILE
    const = lambda shape: pl.BlockSpec(shape, lambda i: (0,) * len(shape))
    return pl.pallas_call(
        functools.partial(_pool_prompt_kernel, tm=tm),
        grid=(t // tm,),
        in_specs=[pl.BlockSpec((tm, D_MODEL), lambda i: (i, 0)),
                  const((1, 6 * D_MODEL)), const((1, D_MODEL)), const((1, D_MODEL)),
                  const(pool_w.shape), const((1, D_MODEL)),
                  const((N_EXPERTS, D_MODEL)), const((N_EXPERTS, 1))],
        out_specs=[pl.BlockSpec((tm, D_MODEL), lambda i: (i, 0)),
                   pl.BlockSpec((tm, D_MODEL), lambda i: (i, 0)),
                   pl.BlockSpec((N_EXPERTS, tm), lambda i: (0, i)),
                   const((HALO, D_MODEL))],
        out_shape=[jax.ShapeDtypeStruct((t, D_MODEL), F32),
                   jax.ShapeDtypeStruct((t, D_MODEL), BF16),
                   jax.ShapeDtypeStruct((N_EXPERTS, t), F32),
                   jax.ShapeDtypeStruct((HALO, D_MODEL), F32)],
        scratch_shapes=[pltpu.VMEM((tm + HALO, D_MODEL), F32)],
        compiler_params=_cparams("arbitrary"),
        name="pool_mixer_prompt",
    )(x, mod, gmix, gffn, pool_w, pool_scale, wrT, rb)


def _pool_sample_kernel(x_ref, st_ref, mod_ref, gmix_ref, gffn_ref, pw_ref, ps_ref, wrT_ref, rb_ref,
                        x1_ref, h2_ref, gT_ref, pool_ref, *, nb, ds):
    x = x_ref[...]
    mod6 = _split_mod(mod_ref[...])
    hs = _modulate(x, gmix_ref[...], mod6[0], mod6[1])
    slabs = [st_ref[r] for r in range(POOL_BUF)] + [hs[t * nb:(t + 1) * nb] for t in range(ds)]
    for r in range(POOL_BUF):
        pool_ref[r] = slabs[ds + r]
    ys = []
    for g, w in enumerate(POOL_WINDOWS):
        c0 = g * POOL_GROUP
        ds_rows = []
        for t in range(ds):
            p = POOL_BUF + t
            acc = slabs[p][:, c0:c0 + POOL_GROUP]
            for k in range(1, w):
                acc = acc + slabs[p - k][:, c0:c0 + POOL_GROUP]
            ds_rows.append(acc / float(w) - slabs[p][:, c0:c0 + POOL_GROUP])
        d = jnp.concatenate(ds_rows, axis=0)
        ys.append(jnp.dot(d, pw_ref[g], precision=HIGHEST, preferred_element_type=F32))
    y = jnp.concatenate(ys, axis=1) * ps_ref[...]
    _ffn_prologue(x + mod6[2] * y, mod6, gffn_ref, wrT_ref, rb_ref, x1_ref, h2_ref, gT_ref)


def _pool_sample_call(x, state, mod, gmix, gffn, pool_w, pool_scale, wrT, rb, nb, ds):
    t = x.shape[0]
    full = lambda a: pl.BlockSpec(a.shape, lambda: (0,) * a.ndim)
    args = (x, state, mod, gmix, gffn, pool_w, pool_scale, wrT, rb)
    out_shape = [jax.ShapeDtypeStruct((t, D_MODEL), F32),
                 jax.ShapeDtypeStruct((t, D_MODEL), BF16),
                 jax.ShapeDtypeStruct((N_EXPERTS, t), F32),
                 jax.ShapeDtypeStruct((POOL_BUF, nb, D_MODEL), F32)]
    return pl.pallas_call(
        functools.partial(_pool_sample_kernel, nb=nb, ds=ds),
        in_specs=[full(a) for a in args],
        out_specs=[full(s) for s in out_shape],
        out_shape=out_shape,
        compiler_params=pltpu.CompilerParams(vmem_limit_bytes=VMEM_LIMIT),
        name="pool_mixer_sample",
    )(*args)


def _moe_kernel(h_ref, gates_ref, x_ref, g2_ref, wgu_ref, wd_ref, o_ref, acc_ref):
    e = pl.program_id(1)

    @pl.when(e == 0)
    def _():
        acc_ref[...] = jnp.zeros_like(acc_ref)

    gu = jnp.dot(h_ref[...], wgu_ref[0], preferred_element_type=F32)
    gate_in = gu[:, :D_EXPERT]
    a = gate_in * jax.nn.sigmoid(gate_in) * gu[:, D_EXPERT:]
    gates = gates_ref[...]
    lane = lax.broadcasted_iota(jnp.int32, gates.shape, 1)
    gcol = jnp.sum(jnp.where(lane == e, gates, 0.0), axis=1, keepdims=True)
    acc_ref[...] += jnp.dot((a * gcol).astype(BF16), wd_ref[0], preferred_element_type=F32)

    @pl.when(e == N_EXPERTS - 1)
    def _():
        o_ref[...] = x_ref[...] + g2_ref[...] * acc_ref[...]


def _moe_call(h, gates, x, g2, wgu, wd):
    t = h.shape[0]
    tm = min(MOE_ROW_TILE, t)
    g2_rows = g2.shape[0]
    g2_spec = (pl.BlockSpec((1, D_MODEL), lambda i, e: (0, 0)) if g2_rows == 1
               else pl.BlockSpec((tm, D_MODEL), lambda i, e: (i, 0)))
    return pl.pallas_call(
        _moe_kernel,
        grid=(t // tm, N_EXPERTS),
        in_specs=[pl.BlockSpec((tm, D_MODEL), lambda i, e: (i, 0)),
                  pl.BlockSpec((tm, N_EXPERTS), lambda i, e: (i, 0)),
                  pl.BlockSpec((tm, D_MODEL), lambda i, e: (i, 0)),
                  g2_spec,
                  pl.BlockSpec((1, D_MODEL, 2 * D_EXPERT), lambda i, e: (e, 0, 0)),
                  pl.BlockSpec((1, D_EXPERT, D_MODEL), lambda i, e: (e, 0, 0))],
        out_specs=pl.BlockSpec((tm, D_MODEL), lambda i, e: (i, 0)),
        out_shape=jax.ShapeDtypeStruct((t, D_MODEL), F32),
        scratch_shapes=[pltpu.VMEM((tm, D_MODEL), F32)],
        compiler_params=_cparams("arbitrary", "arbitrary"),
        name="moe_dense",
    )(h, gates, x, g2, wgu, wd)


def _moe_routed_kernel(cnt_ref, h_ref, gatesT_ref, x_ref, g2_ref, before_ref, wgu_ref, wd_ref,
                       o_ref, acc_ref, rankT_ref, *, chunk, tail_chunks):
    i = pl.program_id(0)
    e = pl.program_id(1)

    @pl.when(e == 0)
    def _():
        acc_ref[...] = jnp.zeros_like(acc_ref)
        rankT_ref[...] = jnp.dot(jnp.where(gatesT_ref[...] > 0.0, 1.0, 0.0).astype(BF16), before_ref[...],
                                 preferred_element_type=F32)

    g_row = gatesT_ref[pl.ds(e, 1), :]
    rank_row = rankT_ref[pl.ds(e, 1), :]

    def expert_pass(base, rows):
        slot = base.astype(F32) + lax.broadcasted_iota(jnp.int32, (rows, 1), 0).astype(F32)
        take = jnp.logical_and(rank_row == slot, g_row > 0.0)
        sel = jnp.where(take, 1.0, 0.0).astype(BF16)
        xg = jnp.dot(sel, h_ref[...], preferred_element_type=F32).astype(BF16)
        gu = jnp.dot(xg, wgu_ref[0], preferred_element_type=F32)
        gate_in = gu[:, :D_EXPERT]
        a = gate_in * jax.nn.sigmoid(gate_in) * gu[:, D_EXPERT:]
        g_sel = jnp.sum(jnp.where(take, g_row, 0.0), axis=1, keepdims=True)
        y = jnp.dot((a * g_sel).astype(BF16), wd_ref[0], preferred_element_type=F32)
        acc_ref[...] += lax.dot_general(sel, y.astype(BF16), (((0,), (0,)), ((), ())),
                                        preferred_element_type=F32)

    n = cnt_ref[e, i]
    n_full = (n + (chunk - tail_chunks[0] - 1)) // chunk
    left = n - n_full * chunk

    def body(c, carry):
        expert_pass(c * chunk, chunk)
        return carry

    lax.fori_loop(0, n_full, body, 0)
    for rows, lower in zip(tail_chunks, tail_chunks[1:] + (0,)):
        @pl.when(jnp.logical_and(left > lower, left <= rows))
        def _(rows=rows):
            expert_pass(n_full * chunk, rows)

    @pl.when(e == N_EXPERTS - 1)
    def _():
        o_ref[...] = x_ref[...] + g2_ref[...] * acc_ref[...]


def _moe_routed_call(h, gatesT, x, g2, wgu, wd):
    t = h.shape[0]
    tm = MOE_ROW_TILE
    n_tiles = t // tm
    counts = jnp.sum((gatesT > 0.0).reshape(N_EXPERTS, n_tiles, tm), axis=-1).astype(jnp.int32)
    before = (jnp.arange(tm)[:, None] < jnp.arange(tm)[None, :]).astype(BF16)
    grid_spec = pltpu.PrefetchScalarGridSpec(
        num_scalar_prefetch=1,
        grid=(n_tiles, N_EXPERTS),
        in_specs=[pl.BlockSpec((tm, D_MODEL), lambda i, e, cnt: (i, 0)),
                  pl.BlockSpec((N_EXPERTS, tm), lambda i, e, cnt: (0, i)),
                  pl.BlockSpec((tm, D_MODEL), lambda i, e, cnt: (i, 0)),
                  pl.BlockSpec((1, D_MODEL), lambda i, e, cnt: (0, 0)),
                  pl.BlockSpec((tm, tm), lambda i, e, cnt: (0, 0)),
                  pl.BlockSpec((1, D_MODEL, 2 * D_EXPERT), lambda i, e, cnt: (e, 0, 0)),
                  pl.BlockSpec((1, D_EXPERT, D_MODEL), lambda i, e, cnt: (e, 0, 0))],
        out_specs=pl.BlockSpec((tm, D_MODEL), lambda i, e, cnt: (i, 0)),
        scratch_shapes=[pltpu.VMEM((tm, D_MODEL), F32), pltpu.VMEM((N_EXPERTS, tm), F32)])
    return pl.pallas_call(
        functools.partial(_moe_routed_kernel, chunk=MOE_CHUNK, tail_chunks=MOE_TAIL_CHUNKS),
        grid_spec=grid_spec,
        out_shape=jax.ShapeDtypeStruct((t, D_MODEL), F32),
        compiler_params=_cparams("arbitrary", "arbitrary"),
        name="moe_routed",
    )(counts, h, gatesT, x, g2, before, wgu, wd)


def _seg_norm(t, g_full, seg_ref, segT_ref):
    sq = t * t
    hi = sq.astype(BF16)
    lo = (sq - hi.astype(F32)).astype(BF16)
    ssq = (jnp.dot(hi, seg_ref[...], preferred_element_type=F32)
           + jnp.dot(lo, seg_ref[...], preferred_element_type=F32))
    inv = lax.rsqrt(ssq * (1.0 / HEAD_QK) + RMS_EPS)
    ihi = inv.astype(BF16)
    ilo = (inv - ihi.astype(F32)).astype(BF16)
    inv_full = (jnp.dot(ihi, segT_ref[...], preferred_element_type=F32)
                + jnp.dot(ilo, segT_ref[...], preferred_element_type=F32))
    return t * inv_full * g_full


def _qkv_kernel(x_ref, mod_ref, gmix_ref, w_ref, gq_ref, gk_ref, seg_ref, segT_ref, *out_refs,
                prompt):
    mod6 = _split_mod(mod_ref[...])
    h = _modulate(x_ref[...], gmix_ref[...], mod6[0], mod6[1]).astype(BF16)
    qkv = jnp.dot(h, w_ref[...], preferred_element_type=F32)
    q = _seg_norm(qkv[:, :D_MODEL], gq_ref[...], seg_ref, segT_ref)
    k = _seg_norm(qkv[:, D_MODEL:2 * D_MODEL], gk_ref[...], seg_ref, segT_ref)
    v = qkv[:, 2 * D_MODEL:]
    q_ref, k32_ref, v32_ref = out_refs[:3]
    q_ref[...] = (q * (HEAD_QK ** -0.5 * LOG2E)).astype(BF16)
    k32_ref[...] = k
    v32_ref[...] = v
    if prompt:
        kb_ref, vT_ref = out_refs[3:]
        kb_ref[...] = k.astype(BF16)
        vT = v.T.astype(BF16)
        for b in range(vT_ref.shape[0]):
            vT_ref[b] = vT[:, b * ATTN_BK:(b + 1) * ATTN_BK]


def _qkv_call(x, mod, gmix, wqkv, gq, gk, seg, segT, prompt):
    t = x.shape[0]
    tm = min(ROW_TILE, t)
    const = lambda shape: pl.BlockSpec(shape, lambda i: (0,) * len(shape))
    row = lambda: pl.BlockSpec((tm, D_MODEL), lambda i: (i, 0))
    mod_spec = const((1, 6 * D_MODEL)) if mod.shape[0] == 1 else pl.BlockSpec((tm, 6 * D_MODEL), lambda i: (i, 0))
    out_specs = [row(), row(), row()]
    out_shape = [jax.ShapeDtypeStruct((t, D_MODEL), BF16),
                 jax.ShapeDtypeStruct((t, D_MODEL), F32),
                 jax.ShapeDtypeStruct((t, D_MODEL), F32)]
    if prompt:
        out_specs += [row(), pl.BlockSpec((tm // ATTN_BK, D_MODEL, ATTN_BK), lambda i: (i, 0, 0))]
        out_shape += [jax.ShapeDtypeStruct((t, D_MODEL), BF16),
                      jax.ShapeDtypeStruct((t // ATTN_BK, D_MODEL, ATTN_BK), BF16)]
    return pl.pallas_call(
        functools.partial(_qkv_kernel, prompt=prompt),
        grid=(t // tm,),
        in_specs=[row(), mod_spec, const((1, D_MODEL)), const(wqkv.shape),
                  const((1, D_MODEL)), const((1, D_MODEL)), const(seg.shape), const(segT.shape)],
        out_specs=out_specs,
        out_shape=out_shape,
        compiler_params=_cparams("arbitrary"),
        name="qkv_prompt" if prompt else "qkv_sample",
    )(x, mod, gmix, wqkv, gq, gk, seg, segT)


def _diff_lambda(lamv, lam0):
    a = jnp.sum(lamv[0:1] * lamv[1:2], axis=1, keepdims=True)
    b = jnp.sum(lamv[2:3] * lamv[3:4], axis=1, keepdims=True)
    return jnp.exp(a) - jnp.exp(b) + lam0


def _attn_prompt_kernel(slope_ref, q_ref, k_ref, vT_ref, kbias_ref, lamv_ref, sg_ref, o_ref,
                        sa_ref, sb_ref, pa_ref, pb_ref, m_ref, alpha_ref, acc_ref, *, bq, bk, lam0):
    hd = pl.program_id(0)
    qi = pl.program_id(1)
    slope = slope_ref[hd]
    q0 = qi * bq
    per_q = bq // bk

    qT = q_ref[...].astype(F32).T
    row = lax.broadcasted_iota(jnp.int32, qT.shape, 0)
    row2 =lax.broadcasted_iota(jnp.int32, (KEY_DIM, 2 * bq), 0)
    qp = jnp.concatenate(
        [jnp.concatenate([jnp.where(row < HEAD_QK, qT, 0.0), jnp.where(row >= HEAD_QK, qT, 0.0)], axis=1),
         jnp.where(row2 < ALIBI_PARTS, 1.0, 0.0)], axis=0).astype(BF16)
    kbias = kbias_ref[0]
    ones_rows = jnp.ones((DENOM_ROWS, bk), BF16)

    s_bufs, p_bufs = (sa_ref, sb_ref), (pa_ref, pb_ref)

    def shift_of(j):
        return slope * (j * bk - q0).astype(F32)

    def step(i, par, diag, do_scores=True, do_exp=True):
        v_aug = jnp.concatenate([vT_ref[jnp.maximum(i - 2, 0)], ones_rows], axis=0)
        acc_ref[...] = alpha_ref[par] * acc_ref[...] + jnp.dot(v_aug, p_bufs[par][...],
                                                              preferred_element_type=F32)
        m_prev = m_ref[...]
        if do_exp:
            p_bufs[1 - par][...] = jnp.exp2(s_bufs[1 - par][...] - (m_prev - shift_of(i - 1))).astype(BF16)
        if do_scores:
            kblk = k_ref[pl.ds(pl.multiple_of(i * bk, bk), bk), :]
            s = jnp.dot(jnp.concatenate([kblk, kbias], axis=1), qp, preferred_element_type=F32)
            if diag is not None:
                kk = lax.broadcasted_iota(jnp.int32, s.shape, 0) + diag
                qq = lax.broadcasted_iota(jnp.int32, s.shape, 1)
                s = jnp.where(kk <= jnp.where(qq >= bq, qq - bq, qq), s, -jnp.inf)
            s_bufs[par][...] = s
            m_new = jnp.maximum(m_prev, jnp.max(s, axis=0, keepdims=True) + shift_of(i))
            alpha_ref[par] = jnp.exp2(m_prev - m_new)
            m_ref[...] = m_new

    m_ref[...] = jnp.full(m_ref.shape, NEG_BIG, F32)
    acc_ref[...] = jnp.zeros_like(acc_ref)
    alpha_ref[...] = jnp.ones_like(alpha_ref)
    pa_ref[...] = jnp.zeros_like(pa_ref)
    sb_ref[...] = jnp.full(sb_ref.shape, -jnp.inf, F32)
    n_full = qi * per_q

    def pair(t, carry):
        step(2 * t, 0, None)
        step(2 * t + 1, 1, None)
        return carry

    lax.fori_loop(0, n_full // 2, pair, 0)
    for d in range(per_q + 2):
        step(n_full + d, d % 2, d * bk, do_scores=d < per_q, do_exp=d <= per_q)

    acc = acc_ref[...]
    o_all = acc[:HEAD_V] / acc[HEAD_V:HEAD_V + 1]
    lam = _diff_lambda(lamv_ref[...], lam0)
    o = o_all[:, :bq] - lam * o_all[:, bq:]
    ms = jnp.mean(o * o, axis=0, keepdims=True)
    o = o * lax.rsqrt(ms + RMS_EPS) * sg_ref[...] * (1.0 - lam0)
    o_ref[...] = o.T.astype(BF16)


def _attn_prompt_call(slopes_l2, q, kb, vT, kbias, lamv, sg, lam0):
    t = q.shape[0]
    bq, bk = ATTN_BQ, ATTN_BK
    assert bq % (2 * bk) == 0
    nk = t // bk
    acc_rows = HEAD_V + DENOM_ROWS
    return pl.pallas_call(
        functools.partial(_attn_prompt_kernel, bq=bq, bk=bk, lam0=lam0),
        grid=(N_HEADS, t // bq),
        in_specs=[pl.BlockSpec(memory_space=pltpu.SMEM),
                  pl.BlockSpec((bq, KEY_DIM), lambda h, i: (i, h)),
                  pl.BlockSpec((t, KEY_DIM), lambda h, i: (0, h)),
                  pl.BlockSpec((nk, HEAD_V, bk), lambda h, i: (0, h, 0)),
                  pl.BlockSpec((1, bk, KEY_DIM), lambda h, i: (h, 0, 0)),
                  pl.BlockSpec(lamv.shape, lambda h, i: (0, 0)),
                  pl.BlockSpec(sg.shape, lambda h, i: (0, 0))],
        out_specs=pl.BlockSpec((bq, HEAD_V), lambda h, i: (i, h)),
        out_shape=jax.ShapeDtypeStruct((t, D_MODEL), BF16),
        scratch_shapes=[pltpu.VMEM((bk, 2 * bq), F32), pltpu.VMEM((bk, 2 * bq), F32),
                        pltpu.VMEM((bk, 2 * bq), BF16), pltpu.VMEM((bk, 2 * bq), BF16),
                        pltpu.VMEM((1, 2 * bq), F32), pltpu.VMEM((2, 1, 2 * bq), F32),
                        pltpu.VMEM((acc_rows, 2 * bq), F32)],
        compiler_params=_cparams("arbitrary", "arbitrary"),
        name="diff_attn_prompt",
    )(slopes_l2, q, kb, vT, kbias, lamv, sg)


def _attn_sample_kernel(pt_ref, *refs, n_pages, page, ds, lam0):
    k_refs = refs[:n_pages]
    v_refs = refs[n_pages:2 * n_pages]
    (qbd_ref, bias_ref, slope_ref, knew_ref, vnew_ref, bnew_ref, lamv_ref, sg_ref,
     o_ref, m_ref, l_ref, acc_ref) = refs[2 * n_pages:]
    c = pl.program_id(1)
    nc = pl.num_programs(1)
    chunk = n_pages * page

    @pl.when(c == 0)
    def _():
        m_ref[...] = jnp.full(m_ref.shape, NEG_BIG, F32)
        l_ref[...] = jnp.zeros_like(l_ref)
        acc_ref[...] = jnp.zeros_like(acc_ref)

    def update(s, shift, v_bf16):
        m_old = m_ref[...]
        m_new = jnp.maximum(m_old, jnp.max(s, axis=1, keepdims=True) + shift)
        alpha = jnp.exp2(m_old - m_new)
        p = jnp.exp2(s - (m_new - shift))
        l_ref[...] = alpha * l_ref[...] + jnp.sum(p, axis=1, keepdims=True)
        acc_ref[...] = alpha * acc_ref[...] + jnp.dot(p.astype(BF16), v_bf16,
                                                      preferred_element_type=F32)
        m_ref[...] = m_new

    def head_major(refs):
        return jnp.concatenate(
            [jnp.concatenate([r[pl.ds(hd, page, stride=N_HEADS), :].astype(BF16) for r in refs], axis=0)
             for hd in range(N_HEADS)], axis=1)

    sT = jnp.dot(head_major(k_refs), qbd_ref[0], preferred_element_type=F32)
    s = sT.T + bias_ref[...]
    past = nc * chunk
    shift = slope_ref[...] * (c * chunk - past).astype(F32)
    update(s, shift, head_major(v_refs))

    @pl.when(c == nc - 1)
    def _():
        sTn = jnp.dot(knew_ref[0], qbd_ref[0], preferred_element_type=F32)
        sn = sTn.T + bnew_ref[...]
        update(sn, jnp.zeros_like(slope_ref[...]), vnew_ref[0])

        lam = _diff_lambda(lamv_ref[...], lam0)
        acc = acc_ref[...]
        l = l_ref[...]
        rows = 2 * ds
        for hd in range(N_HEADS):
            blk = acc[hd * rows:(hd + 1) * rows, hd * HEAD_V:(hd + 1) * HEAD_V] / l[hd * rows:(hd + 1) * rows]
            o = blk[0:ds] - lam * blk[ds:rows]
            ms = jnp.mean(o * o, axis=1, keepdims=True)
            o = o * lax.rsqrt(ms + RMS_EPS) * sg_ref[...] * (1.0 - lam0)
            o_ref[0, 0:ds, hd * HEAD_V:(hd + 1) * HEAD_V] = o
            o_ref[0, ds:rows, hd * HEAD_V:(hd + 1) * HEAD_V] = jnp.zeros((rows - ds, HEAD_V), F32)


def _attn_sample_call(page_table, cache_k, cache_v, qbd, bias, slope_col, knew, vnew, bnew, lamv, sg_row,
                      lam0, ds):
    nb, n_pages_total = page_table.shape
    page = cache_k.shape[1] // N_HEADS
    n_pages = PAGES_PER_STEP
    chunk = n_pages * page
    rows = 2 * ds
    page_spec = lambda i: pl.BlockSpec((None, page * N_HEADS, KEY_DIM),
                                       lambda b, c, pt, i=i: (pt[b, c * n_pages + i], 0, 0))
    const2 = lambda a: pl.BlockSpec(a.shape, lambda b, c, pt: (0, 0))
    per_seq = lambda a: pl.BlockSpec((1,) + a.shape[1:], lambda b, c, pt: (b, 0, 0))
    grid_spec = pltpu.PrefetchScalarGridSpec(
        num_scalar_prefetch=1,
        grid=(nb, n_pages_total // n_pages),
        in_specs=([page_spec(i) for i in range(n_pages)] + [page_spec(i) for i in range(n_pages)]
                  + [per_seq(qbd), const2(bias), const2(slope_col), per_seq(knew), per_seq(vnew),
                     const2(bnew), const2(lamv), const2(sg_row)]),
        out_specs=pl.BlockSpec((1, rows, D_MODEL), lambda b, c, pt: (b, 0, 0)),
        scratch_shapes=[pltpu.VMEM((LANES, 1), F32), pltpu.VMEM((LANES, 1), F32),
                        pltpu.VMEM((LANES, D_MODEL), F32)])
    return pl.pallas_call(
        functools.partial(_attn_sample_kernel, n_pages=n_pages, page=page, ds=ds, lam0=lam0),
        grid_spec=grid_spec,
        out_shape=jax.ShapeDtypeStruct((nb, rows, D_MODEL), F32),
        compiler_params=_cparams("arbitrary", "arbitrary"),
        name="diff_attn_sample",
    )(page_table, *([cache_k] * n_pages), *([cache_v] * n_pages), qbd, bias, slope_col, knew, vnew,
      bnew, lamv, sg_row)


def _oproj_kernel(o_ref, x_ref, mod_ref, wo_ref, gffn_ref, wrT_ref, rb_ref, x1_ref, h2_ref, gT_ref):
    mod6 = _split_mod(mod_ref[...])
    y = jnp.dot(o_ref[...], wo_ref[...], preferred_element_type=F32)
    _ffn_prologue(x_ref[...] + mod6[2] * y, mod6, gffn_ref, wrT_ref, rb_ref, x1_ref, h2_ref, gT_ref)


def _oproj_call(o, x, mod, wo, gffn, wrT, rb):
    t = x.shape[0]
    tm = min(ROW_TILE, t)
    const = lambda shape: pl.BlockSpec(shape, lambda i: (0,) * len(shape))
    row = lambda: pl.BlockSpec((tm, D_MODEL), lambda i: (i, 0))
    mod_spec = const((1, 6 * D_MODEL)) if mod.shape[0] == 1 else pl.BlockSpec((tm, 6 * D_MODEL), lambda i: (i, 0))
    return pl.pallas_call(
        _oproj_kernel,
        grid=(t // tm,),
        in_specs=[row(), row(), mod_spec, const(wo.shape), const((1, D_MODEL)),
                  const((N_EXPERTS, D_MODEL)), const((N_EXPERTS, 1))],
        out_specs=[row(), row(), pl.BlockSpec((N_EXPERTS, tm), lambda i: (0, i))],
        out_shape=[jax.ShapeDtypeStruct((t, D_MODEL), F32),
                   jax.ShapeDtypeStruct((t, D_MODEL), BF16),
                   jax.ShapeDtypeStruct((N_EXPERTS, t), F32)],
        compiler_params=_cparams("arbitrary"),
        name="attn_out_proj",
    )(o, x, mod, wo, gffn, wrT, rb)


def kernel(x_prompt, x_sample, state_pool, cache_k, cache_v, page_table, c_prompt, c_sample,
           w_mod, b_mod, norm_mix_g, norm_ffn_g, pool_w, pool_scale, w_qkv, q_norm_g, k_norm_g,
           lambda_q1, lambda_k1, lambda_q2, lambda_k2, subln_g, w_o, w_router, router_bias,
           w_gate, w_up, w_down):
    assert DEPTH == 2 and x_prompt.shape[0] == 1
    seq = x_prompt.shape[1]
    nb, ds, _ = x_sample.shape
    n_phys, page = cache_k.shape[1], cache_k.shape[2]
    past = page_table.shape[1] * page
    ts = nb * ds

    c_all = jnp.concatenate([c_prompt, jnp.zeros((7, D_MODEL), F32), c_sample], axis=0)
    mods = _mod_call(c_all, w_mod, b_mod)
    mod_p = mods[:, 0:1]
    mod_s = jnp.tile(mods[:, 8:8 + nb], (1, ds, 1))

    pool_w_b = pool_w.astype(BF16)
    wgu = jnp.concatenate([w_gate, w_up], axis=-1).astype(BF16)
    wd = w_down.astype(BF16)
    wqkv = w_qkv.astype(BF16)
    wo = w_o.astype(BF16)
    wrT = w_router.T
    rb = router_bias.reshape(N_EXPERTS, 1)
    gq = jnp.tile(q_norm_g.reshape(1, 1, KEY_DIM), (1, N_HEADS, 1)).reshape(-1, 1, D_MODEL)
    gk = jnp.tile(k_norm_g.reshape(1, 1, KEY_DIM), (1, N_HEADS, 1)).reshape(-1, 1, D_MODEL)
    seg_id = jnp.arange(D_MODEL) // HEAD_QK
    seg = (seg_id[:, None] == jnp.arange(LANES)[None, :]).astype(BF16)
    segT = seg.T
    slopes = 2.0 ** (-8.0 * jnp.arange(1, N_HEADS + 1, dtype=F32) / N_HEADS)
    slopes_l2 = slopes * LOG2E

    xs = x_sample.transpose(1, 0, 2).reshape(ts, D_MODEL)
    state = state_pool.transpose(0, 2, 1, 3)

    x1p, h2p, gTp, pst = _pool_prompt_call(x_prompt[0], mod_p[0], norm_mix_g[0:1], norm_ffn_g[0:1],
                                           pool_w_b[0], pool_scale[0:1], wrT, rb)
    x1s, h2s, gTs, pool_s = _pool_sample_call(xs, state[0], mod_s[0], norm_mix_g[0:1], norm_ffn_g[0:1],
                                              pool_w[0], pool_scale[0:1], wrT, rb, nb, ds)
    xp = _moe_routed_call(h2p, gTp, x1p, mod_p[0][:, 5 * D_MODEL:], wgu[0], wd[0])
    xs = _moe_call(h2s, gTs.T, x1s, mod_s[0][:, 5 * D_MODEL:], wgu[0], wd[0])

    lam0 = _lambda_init(1)
    lamv = jnp.stack([lambda_q1[0], lambda_k1[0], lambda_q2[0], lambda_k2[0]])
    qb, k32, v32, kb, vT = _qkv_call(xp, mod_p[1], norm_mix_g[1:2], wqkv[0], gq[0], gk[0], seg, segT, True)
    rest = slopes_l2[:, None] * jnp.arange(ATTN_BK, dtype=F32)[None, :]
    parts = []
    for _ in range(ALIBI_PARTS):
        part = _truncate_to_bf16(rest)
        parts.append(part)
        rest = rest - part
    kbias = jnp.pad(jnp.stack(parts, axis=-1).astype(BF16), ((0, 0), (0, 0), (0, KEY_DIM - ALIBI_PARTS)))
    o_p = _attn_prompt_call(slopes_l2, qb, kb, vT, kbias, lamv, subln_g[0].reshape(HEAD_V, 1), lam0)
    x1p, h2p, gTp = _oproj_call(o_p, xp, mod_p[1], wo[0], norm_ffn_g[1:2], wrT, rb)
    yp = _moe_routed_call(h2p, gTp, x1p, mod_p[1][:, 5 * D_MODEL:], wgu[1], wd[1])

    qs, ks32, vs32 = _qkv_call(xs, mod_s[1], norm_mix_g[1:2], wqkv[0], gq[0], gk[0], seg, segT, False)
    rows = 2 * ds
    q_seq = qs.astype(F32).reshape(ds, nb, D_MODEL).transpose(1, 2, 0)
    colmask = (jnp.arange(D_MODEL)[:, None] // HEAD_QK) == (jnp.arange(N_HEADS * rows)[None, :] // ds)
    qbd = jnp.where(colmask[None], jnp.tile(q_seq, (1, 1, 2 * N_HEADS)), 0.0)
    qbd = jnp.pad(qbd, ((0, 0), (0, 0), (0, LANES - N_HEADS * rows))).astype(BF16)
    row_slope = jnp.pad(jnp.repeat(slopes_l2, rows), (0, LANES - N_HEADS * rows))
    chunk = PAGES_PER_STEP * page
    bias_s = row_slope[:, None] * jnp.arange(chunk, dtype=F32)[None, :]
    jn = jnp.arange(LANES)[None, :]
    tok = (jnp.arange(LANES) % ds)[:, None]
    bias_new = jnp.where(jnp.logical_and(jn <= tok, jn < ds), row_slope[:, None] * jn.astype(F32), -jnp.inf)
    to_seq = lambda a: jnp.pad(a.reshape(ds, nb, D_MODEL).transpose(1, 0, 2),
                               ((0, 0), (0, LANES - ds), (0, 0))).astype(BF16)
    o_s = _attn_sample_call(page_table, cache_k.reshape(-1, page * N_HEADS, KEY_DIM),
                            cache_v.reshape(-1, page * N_HEADS, HEAD_V), qbd, bias_s,
                            row_slope.reshape(LANES, 1), to_seq(ks32), to_seq(vs32), bias_new, lamv,
                            subln_g[0].reshape(1, HEAD_V), lam0, ds)
    o_s = o_s[:, :ds].transpose(1, 0, 2).reshape(ts, D_MODEL).astype(BF16)
    x1s, h2s, gTs = _oproj_call(o_s, xs, mod_s[1], wo[0], norm_ffn_g[1:2], wrT, rb)
    ys = _moe_call(h2s, gTs.T, x1s, mod_s[1][:, 5 * D_MODEL:], wgu[1], wd[1])

    seq_major = lambda a: a.reshape(ds, nb, D_MODEL).transpose(1, 0, 2)
    return (yp[None],
            seq_major(ys),
            pst[None, None, 1:],
            pool_s.transpose(1, 0, 2)[None],
            k32.reshape(1, 1, seq // page, page, N_HEADS, KEY_DIM),
            v32.reshape(1, 1, seq // page, page, N_HEADS, HEAD_V),
            seq_major(ks32).reshape(1, nb, ds, N_HEADS, KEY_DIM),
            seq_major(vs32).reshape(1, nb, ds, N_HEADS, HEAD_V))
```

```python
import functools
import math

import jax
import jax.numpy as jnp
from jax import lax
from jax.experimental import pallas as pl
from jax.experimental.pallas import tpu as pltpu

F32 = jnp.float32
BF16 = jnp.bfloat16
HIGHEST = lax.Precision.HIGHEST

D_MODEL = 1024
DEPTH = 2
POOL_WINDOWS = (2, 4, 8, 16)
POOL_GROUP = D_MODEL // len(POOL_WINDOWS)
POOL_BUF = max(POOL_WINDOWS) - 1
HALO = POOL_BUF + 1
N_HEADS = 8
HEAD_QK = 64
HEAD_V = 2 * HEAD_QK
KEY_DIM = 2 * HEAD_QK
N_EXPERTS = 16
EXPERTS_PER_GROUP = 4
N_GROUPS = N_EXPERTS // EXPERTS_PER_GROUP
D_EXPERT = D_MODEL // 2
RMS_EPS = 1e-6
LOG2E = 1.4426950408889634
LANES = 128
NEG_BIG = -1e30

ROW_TILE = 512
MOE_ROW_TILE = 1024
MOE_CHUNK = 256
MOE_TAIL_CHUNKS = (128,)
ATTN_BQ = 1024
ATTN_BK = 256
ALIBI_PARTS = 3
DENOM_ROWS = 16
PAGES_PER_STEP = 16
VMEM_LIMIT = 56 * 1024 * 1024


def _cparams(*sem):
    return pltpu.CompilerParams(dimension_semantics=sem, vmem_limit_bytes=VMEM_LIMIT)


def _lambda_init(layer):
    return 0.8 - 0.6 * math.exp(-0.3 * layer)


def _truncate_to_bf16(x):
    bits = lax.bitcast_convert_type(x, jnp.uint32) & jnp.uint32(0xFFFF0000)
    return lax.bitcast_convert_type(bits, F32)


def _rms(x, g):
    ms = jnp.mean(x * x, axis=-1, keepdims=True)
    return x * lax.rsqrt(ms + RMS_EPS) * g


def _modulate(x, g, shift, scale):
    return _rms(x, g) * (1.0 + scale) + shift


def _split_mod(mod):
    return [mod[:, k * D_MODEL:(k + 1) * D_MODEL] for k in range(6)]


def _route_gates(h, wrT_ref, rb_ref, gT_ref):
    logits = lax.dot_general(wrT_ref[...], h, (((1,), (1,)), ((), ())),
                             precision=HIGHEST, preferred_element_type=F32)
    mx = jnp.max(logits, axis=0, keepdims=True)
    ex = jnp.exp(logits - mx)
    probs = ex / jnp.sum(ex, axis=0, keepdims=True)
    sel = probs + rb_ref[...]
    srow = [sel[e:e + 1] for e in range(N_EXPERTS)]
    prow = [probs[e:e + 1] for e in range(N_EXPERTS)]

    gscore = []
    for g in range(N_GROUPS):
        r = srow[g * EXPERTS_PER_GROUP:(g + 1) * EXPERTS_PER_GROUP]
        best = None
        for a in range(EXPERTS_PER_GROUP):
            for b in range(a + 1, EXPERTS_PER_GROUP):
                s = r[a] + r[b]
                best = s if best is None else jnp.maximum(best, s)
        gscore.append(best)
    gbest = functools.reduce(jnp.maximum, gscore)
    in_grp, taken = [], None
    for g in range(N_GROUPS):
        hit = gscore[g] == gbest
        if taken is not None:
            hit = jnp.logical_and(hit, jnp.logical_not(taken))
        taken = hit if taken is None else jnp.logical_or(taken, hit)
        in_grp.append(hit)

    def pick(rows, k):
        out = rows[(N_GROUPS - 1) * EXPERTS_PER_GROUP + k]
        for g in range(N_GROUPS - 2, -1, -1):
            out = jnp.where(in_grp[g], rows[g * EXPERTS_PER_GROUP + k], out)
        return out

    v = [pick(srow, k) for k in range(EXPERTS_PER_GROUP)]
    p = [pick(prow, k) for k in range(EXPERTS_PER_GROUP)]

    def first_hits(vals, target):
        hits, seen = [], None
        for x in vals:
            hit = x == target
            if seen is not None:
                hit = jnp.logical_and(hit, jnp.logical_not(seen))
            seen = hit if seen is None else jnp.logical_or(seen, hit)
            hits.append(hit)
        return hits

    top1 = first_hits(v, functools.reduce(jnp.maximum, v))
    rest = [jnp.where(top1[k], -jnp.inf, v[k]) for k in range(EXPERTS_PER_GROUP)]
    top2 = first_hits(rest, functools.reduce(jnp.maximum, rest))
    p1 = functools.reduce(jnp.add, [jnp.where(top1[k], p[k], 0.0) for k in range(EXPERTS_PER_GROUP)])
    p2 = functools.reduce(jnp.add, [jnp.where(top2[k], p[k], 0.0) for k in range(EXPERTS_PER_GROUP)])
    den = p1 + p2
    w1, w2 = p1 / den, p2 / den
    for g in range(N_GROUPS):
        for k in range(EXPERTS_PER_GROUP):
            val = jnp.where(top1[k], w1, jnp.where(top2[k], w2, 0.0))
            e = g * EXPERTS_PER_GROUP + k
            gT_ref[e:e + 1, :] = jnp.where(in_grp[g], val, 0.0)


def _ffn_prologue(x1, mod6, gffn_ref, wrT_ref, rb_ref, x1_ref, h2_ref, gT_ref):
    x1_ref[...] = x1
    h2 = _modulate(x1, gffn_ref[...], mod6[3], mod6[4])
    h2_ref[...] = h2.astype(BF16)
    _route_gates(h2, wrT_ref, rb_ref, gT_ref)


def _mod_kernel(c_ref, w_ref, b_ref, o_ref):
    c = c_ref[...]
    a = c * jax.nn.sigmoid(c)
    o_ref[0] = jnp.dot(a, w_ref[0], precision=HIGHEST, preferred_element_type=F32) + b_ref[0]


def _mod_call(c_all, w_mod, b_mod):
    rows = c_all.shape[0]
    tn = 1536
    return pl.pallas_call(
        _mod_kernel,
        grid=(DEPTH, 6 * D_MODEL // tn),
        in_specs=[pl.BlockSpec((rows, D_MODEL), lambda i, j: (0, 0)),
                  pl.BlockSpec((1, D_MODEL, tn), lambda i, j: (i, 0, j)),
                  pl.BlockSpec((1, 1, tn), lambda i, j: (i, 0, j))],
        out_specs=pl.BlockSpec((1, rows, tn), lambda i, j: (i, 0, j)),
        out_shape=jax.ShapeDtypeStruct((DEPTH, rows, 6 * D_MODEL), F32),
        compiler_params=_cparams("arbitrary", "arbitrary"),
        name="adaln_mod",
    )(c_all, w_mod, b_mod.reshape(DEPTH, 1, 6 * D_MODEL))


def _pool_prompt_kernel(x_ref, mod_ref, gmix_ref, gffn_ref, pw_ref, ps_ref, wrT_ref, rb_ref,
                        x1_ref, h2_ref, gT_ref, pst_ref, ext_ref, *, tm):
    i = pl.program_id(0)

    @pl.when(i == 0)
    def _():
        ext_ref[0:HALO, :] = jnp.zeros((HALO, D_MODEL), F32)

    x = x_ref[...]
    mod6 = _split_mod(mod_ref[...])
    ext_ref[HALO:HALO + tm, :] = _modulate(x, gmix_ref[...], mod6[0], mod6[1])
    pos = i * tm + lax.broadcasted_iota(jnp.int32, (tm, 1), 0)
    ys = []
    for g, w in enumerate(POOL_WINDOWS):
        c0 = g * POOL_GROUP
        cur = ext_ref[HALO:HALO + tm, c0:c0 + POOL_GROUP]
        acc = cur
        for k in range(1, w):
            acc = acc + ext_ref[HALO - k:HALO - k + tm, c0:c0 + POOL_GROUP]
        cnt = jnp.minimum(pos + 1, w).astype(F32)
        d = acc / cnt - cur
        ys.append(jnp.dot(d.astype(BF16), pw_ref[g], preferred_element_type=F32))
    y = jnp.concatenate(ys, axis=1) * ps_ref[...]
    tail = ext_ref[tm:tm + HALO, :]
    ext_ref[0:HALO, :] = tail
    pst_ref[...] = tail
    _ffn_prologue(x + mod6[2] * y, mod6, gffn_ref, wrT_ref, rb_ref, x1_ref, h2_ref, gT_ref)


def _pool_prompt_call(x, mod, gmix, gffn, pool_w, pool_scale, wrT, rb):
    t = x.shape[0]
    tm = ROW_TILE
    const = lambda shape: pl.BlockSpec(shape, lambda i: (0,) * len(shape))
    return pl.pallas_call(
        functools.partial(_pool_prompt_kernel, tm=tm),
        grid=(t // tm,),
        in_specs=[pl.BlockSpec((tm, D_MODEL), lambda i: (i, 0)),
                  const((1, 6 * D_MODEL)), const((1, D_MODEL)), const((1, D_MODEL)),
                  const(pool_w.shape), const((1, D_MODEL)),
                  const((N_EXPERTS, D_MODEL)), const((N_EXPERTS, 1))],
        out_specs=[pl.BlockSpec((tm, D_MODEL), lambda i: (i, 0)),
                   pl.BlockSpec((tm, D_MODEL), lambda i: (i, 0)),
                   pl.BlockSpec((N_EXPERTS, tm), lambda i: (0, i)),
                   const((HALO, D_MODEL))],
        out_shape=[jax.ShapeDtypeStruct((t, D_MODEL), F32),
                   jax.ShapeDtypeStruct((t, D_MODEL), BF16),
                   jax.ShapeDtypeStruct((N_EXPERTS, t), F32),
                   jax.ShapeDtypeStruct((HALO, D_MODEL), F32)],
        scratch_shapes=[pltpu.VMEM((tm + HALO, D_MODEL), F32)],
        compiler_params=_cparams("arbitrary"),
        name="pool_mixer_prompt",
    )(x, mod, gmix, gffn, pool_w, pool_scale, wrT, rb)


def _pool_sample_kernel(x_ref, st_ref, mod_ref, gmix_ref, gffn_ref, pw_ref, ps_ref, wrT_ref, rb_ref,
                        x1_ref, h2_ref, gT_ref, pool_ref, *, nb, ds):
    x = x_ref[...]
    mod6 = _split_mod(mod_ref[...])
    hs = _modulate(x, gmix_ref[...], mod6[0], mod6[1])
    slabs = [st_ref[r] for r in range(POOL_BUF)] + [hs[t * nb:(t + 1) * nb] for t in range(ds)]
    for r in range(POOL_BUF):
        pool_ref[r] = slabs[ds + r]
    ys = []
    for g, w in enumerate(POOL_WINDOWS):
        c0 = g * POOL_GROUP
        ds_rows = []
        for t in range(ds):
            p = POOL_BUF + t
            acc = slabs[p][:, c0:c0 + POOL_GROUP]
            for k in range(1, w):
                acc = acc + slabs[p - k][:, c0:c0 + POOL_GROUP]
            ds_rows.append(acc / float(w) - slabs[p][:, c0:c0 + POOL_GROUP])
        d = jnp.concatenate(ds_rows, axis=0)
        ys.append(jnp.dot(d, pw_ref[g], precision=HIGHEST, preferred_element_type=F32))
    y = jnp.concatenate(ys, axis=1) * ps_ref[...]
    _ffn_prologue(x + mod6[2] * y, mod6, gffn_ref, wrT_ref, rb_ref, x1_ref, h2_ref, gT_ref)


def _pool_sample_call(x, state, mod, gmix, gffn, pool_w, pool_scale, wrT, rb, nb, ds):
    t = x.shape[0]
    full = lambda a: pl.BlockSpec(a.shape, lambda: (0,) * a.ndim)
    args = (x, state, mod, gmix, gffn, pool_w, pool_scale, wrT, rb)
    out_shape = [jax.ShapeDtypeStruct((t, D_MODEL), F32),
                 jax.ShapeDtypeStruct((t, D_MODEL), BF16),
                 jax.ShapeDtypeStruct((N_EXPERTS, t), F32),
                 jax.ShapeDtypeStruct((POOL_BUF, nb, D_MODEL), F32)]
    return pl.pallas_call(
        functools.partial(_pool_sample_kernel, nb=nb, ds=ds),
        in_specs=[full(a) for a in args],
        out_specs=[full(s) for s in out_shape],
        out_shape=out_shape,
        compiler_params=pltpu.CompilerParams(vmem_limit_bytes=VMEM_LIMIT),
        name="pool_mixer_sample",
    )(*args)


def _moe_kernel(h_ref, gates_ref, x_ref, g2_ref, wgu_ref, wd_ref, o_ref, acc_ref):
    e = pl.program_id(1)

    @pl.when(e == 0)
    def _():
        acc_ref[...] = jnp.zeros_like(acc_ref)

    gu = jnp.dot(h_ref[...], wgu_ref[0], preferred_element_type=F32)
    gate_in = gu[:, :D_EXPERT]
    a = gate_in * jax.nn.sigmoid(gate_in) * gu[:, D_EXPERT:]
    gates = gates_ref[...]
    lane = lax.broadcasted_iota(jnp.int32, gates.shape, 1)
    gcol = jnp.sum(jnp.where(lane == e, gates, 0.0), axis=1, keepdims=True)
    acc_ref[...] += jnp.dot((a * gcol).astype(BF16), wd_ref[0], preferred_element_type=F32)

    @pl.when(e == N_EXPERTS - 1)
    def _():
        o_ref[...] = x_ref[...] + g2_ref[...] * acc_ref[...]


def _moe_call(h, gates, x, g2, wgu, wd):
    t = h.shape[0]
    tm = min(MOE_ROW_TILE, t)
    g2_rows = g2.shape[0]
    g2_spec = (pl.BlockSpec((1, D_MODEL), lambda i, e: (0, 0)) if g2_rows == 1
               else pl.BlockSpec((tm, D_MODEL), lambda i, e: (i, 0)))
    return pl.pallas_call(
        _moe_kernel,
        grid=(t // tm, N_EXPERTS),
        in_specs=[pl.BlockSpec((tm, D_MODEL), lambda i, e: (i, 0)),
                  pl.BlockSpec((tm, N_EXPERTS), lambda i, e: (i, 0)),
                  pl.BlockSpec((tm, D_MODEL), lambda i, e: (i, 0)),
                  g2_spec,
                  pl.BlockSpec((1, D_MODEL, 2 * D_EXPERT), lambda i, e: (e, 0, 0)),
                  pl.BlockSpec((1, D_EXPERT, D_MODEL), lambda i, e: (e, 0, 0))],
        out_specs=pl.BlockSpec((tm, D_MODEL), lambda i, e: (i, 0)),
        out_shape=jax.ShapeDtypeStruct((t, D_MODEL), F32),
        scratch_shapes=[pltpu.VMEM((tm, D_MODEL), F32)],
        compiler_params=_cparams("arbitrary", "arbitrary"),
        name="moe_dense",
    )(h, gates, x, g2, wgu, wd)


def _moe_routed_kernel(cnt_ref, h_ref, gatesT_ref, x_ref, g2_ref, before_ref, wgu_ref, wd_ref,
                       o_ref, acc_ref, rankT_ref, *, chunk, tail_chunks):
    i = pl.program_id(0)
    e = pl.program_id(1)

    @pl.when(e == 0)
    def _():
        acc_ref[...] = jnp.zeros_like(acc_ref)
        rankT_ref[...] = jnp.dot(jnp.where(gatesT_ref[...] > 0.0, 1.0, 0.0).astype(BF16), before_ref[...],
                                 preferred_element_type=F32)

    g_row = gatesT_ref[pl.ds(e, 1), :]
    rank_row = rankT_ref[pl.ds(e, 1), :]

    def expert_pass(base, rows):
        slot = base.astype(F32) + lax.broadcasted_iota(jnp.int32, (rows, 1), 0).astype(F32)
        take = jnp.logical_and(rank_row == slot, g_row > 0.0)
        sel = jnp.where(take, 1.0, 0.0).astype(BF16)
        xg = jnp.dot(sel, h_ref[...], preferred_element_type=F32).astype(BF16)
        gu = jnp.dot(xg, wgu_ref[0], preferred_element_type=F32)
        gate_in = gu[:, :D_EXPERT]
        a = gate_in * jax.nn.sigmoid(gate_in) * gu[:, D_EXPERT:]
        g_sel = jnp.sum(jnp.where(take, g_row, 0.0), axis=1, keepdims=True)
        y = jnp.dot((a * g_sel).astype(BF16), wd_ref[0], preferred_element_type=F32)
        acc_ref[...] += lax.dot_general(sel, y.astype(BF16), (((0,), (0,)), ((), ())),
                                        preferred_element_type=F32)

    n = cnt_ref[e, i]
    n_full = (n + (chunk - tail_chunks[0] - 1)) // chunk
    left = n - n_full * chunk

    def body(c, carry):
        expert_pass(c * chunk, chunk)
        return carry

    lax.fori_loop(0, n_full, body, 0)
    for rows, lower in zip(tail_chunks, tail_chunks[1:] + (0,)):
        @pl.when(jnp.logical_and(left > lower, left <= rows))
        def _(rows=rows):
            expert_pass(n_full * chunk, rows)

    @pl.when(e == N_EXPERTS - 1)
    def _():
        o_ref[...] = x_ref[...] + g2_ref[...] * acc_ref[...]


def _moe_routed_call(h, gatesT, x, g2, wgu, wd):
    t = h.shape[0]
    tm = MOE_ROW_TILE
    n_tiles = t // tm
    counts = jnp.sum((gatesT > 0.0).reshape(N_EXPERTS, n_tiles, tm), axis=-1).astype(jnp.int32)
    before = (jnp.arange(tm)[:, None] < jnp.arange(tm)[None, :]).astype(BF16)
    grid_spec = pltpu.PrefetchScalarGridSpec(
        num_scalar_prefetch=1,
        grid=(n_tiles, N_EXPERTS),
        in_specs=[pl.BlockSpec((tm, D_MODEL), lambda i, e, cnt: (i, 0)),
                  pl.BlockSpec((N_EXPERTS, tm), lambda i, e, cnt: (0, i)),
                  pl.BlockSpec((tm, D_MODEL), lambda i, e, cnt: (i, 0)),
                  pl.BlockSpec((1, D_MODEL), lambda i, e, cnt: (0, 0)),
                  pl.BlockSpec((tm, tm), lambda i, e, cnt: (0, 0)),
                  pl.BlockSpec((1, D_MODEL, 2 * D_EXPERT), lambda i, e, cnt: (e, 0, 0)),
                  pl.BlockSpec((1, D_EXPERT, D_MODEL), lambda i, e, cnt: (e, 0, 0))],
        out_specs=pl.BlockSpec((tm, D_MODEL), lambda i, e, cnt: (i, 0)),
        scratch_shapes=[pltpu.VMEM((tm, D_MODEL), F32), pltpu.VMEM((N_EXPERTS, tm), F32)])
    return pl.pallas_call(
        functools.partial(_moe_routed_kernel, chunk=MOE_CHUNK, tail_chunks=MOE_TAIL_CHUNKS),
        grid_spec=grid_spec,
        out_shape=jax.ShapeDtypeStruct((t, D_MODEL), F32),
        compiler_params=_cparams("arbitrary", "arbitrary"),
        name="moe_routed",
    )(counts, h, gatesT, x, g2, before, wgu, wd)


def _seg_norm(t, g_full, seg_ref, segT_ref):
    sq = t * t
    hi = sq.astype(BF16)
    lo = (sq - hi.astype(F32)).astype(BF16)
    ssq = (jnp.dot(hi, seg_ref[...], preferred_element_type=F32)
           + jnp.dot(lo, seg_ref[...], preferred_element_type=F32))
    inv = lax.rsqrt(ssq * (1.0 / HEAD_QK) + RMS_EPS)
    ihi = inv.astype(BF16)
    ilo = (inv - ihi.astype(F32)).astype(BF16)
    inv_full = (jnp.dot(ihi, segT_ref[...], preferred_element_type=F32)
                + jnp.dot(ilo, segT_ref[...], preferred_element_type=F32))
    return t * inv_full * g_full


def _qkv_kernel(x_ref, mod_ref, gmix_ref, w_ref, gq_ref, gk_ref, seg_ref, segT_ref, *out_refs,
                prompt):
    mod6 = _split_mod(mod_ref[...])
    h = _modulate(x_ref[...], gmix_ref[...], mod6[0], mod6[1]).astype(BF16)
    qkv = jnp.dot(h, w_ref[...], preferred_element_type=F32)
    q = _seg_norm(qkv[:, :D_MODEL], gq_ref[...], seg_ref, segT_ref)
    k = _seg_norm(qkv[:, D_MODEL:2 * D_MODEL], gk_ref[...], seg_ref, segT_ref)
    v = qkv[:, 2 * D_MODEL:]
    q_ref, k32_ref, v32_ref = out_refs[:3]
    q_ref[...] = (q * (HEAD_QK ** -0.5 * LOG2E)).astype(BF16)
    k32_ref[...] = k
    v32_ref[...] = v
    if prompt:
        kb_ref, vT_ref = out_refs[3:]
        kb_ref[...] = k.astype(BF16)
        vT = v.T.astype(BF16)
        for b in range(vT_ref.shape[0]):
            vT_ref[b] = vT[:, b * ATTN_BK:(b + 1) * ATTN_BK]


def _qkv_call(x, mod, gmix, wqkv, gq, gk, seg, segT, prompt):
    t = x.shape[0]
    tm = min(ROW_TILE, t)
    const = lambda shape: pl.BlockSpec(shape, lambda i: (0,) * len(shape))
    row = lambda: pl.BlockSpec((tm, D_MODEL), lambda i: (i, 0))
    mod_spec = const((1, 6 * D_MODEL)) if mod.shape[0] == 1 else pl.BlockSpec((tm, 6 * D_MODEL), lambda i: (i, 0))
    out_specs = [row(), row(), row()]
    out_shape = [jax.ShapeDtypeStruct((t, D_MODEL), BF16),
                 jax.ShapeDtypeStruct((t, D_MODEL), F32),
                 jax.ShapeDtypeStruct((t, D_MODEL), F32)]
    if prompt:
        out_specs += [row(), pl.BlockSpec((tm // ATTN_BK, D_MODEL, ATTN_BK), lambda i: (i, 0, 0))]
        out_shape += [jax.ShapeDtypeStruct((t, D_MODEL), BF16),
                      jax.ShapeDtypeStruct((t // ATTN_BK, D_MODEL, ATTN_BK), BF16)]
    return pl.pallas_call(
        functools.partial(_qkv_kernel, prompt=prompt),
        grid=(t // tm,),
        in_specs=[row(), mod_spec, const((1, D_MODEL)), const(wqkv.shape),
                  const((1, D_MODEL)), const((1, D_MODEL)), const(seg.shape), const(segT.shape)],
        out_specs=out_specs,
        out_shape=out_shape,
        compiler_params=_cparams("arbitrary"),
        name="qkv_prompt" if prompt else "qkv_sample",
    )(x, mod, gmix, wqkv, gq, gk, seg, segT)


def _diff_lambda(lamv, lam0):
    a = jnp.sum(lamv[0:1] * lamv[1:2], axis=1, keepdims=True)
    b = jnp.sum(lamv[2:3] * lamv[3:4], axis=1, keepdims=True)
    return jnp.exp(a) - jnp.exp(b) + lam0


def _attn_prompt_kernel(slope_ref, q_ref, k_ref, vT_ref, kbias_ref, lamv_ref, sg_ref, o_ref,
                        sa_ref, sb_ref, pa_ref, pb_ref, m_ref, alpha_ref, acc_ref, *, bq, bk, lam0):
    hd = pl.program_id(0)
    qi = pl.program_id(1)
    slope = slope_ref[hd]
    q0 = qi * bq
    per_q = bq // bk

    qT = q_ref[...].astype(F32).T
    row = lax.broadcasted_iota(jnp.int32, qT.shape, 0)
    row2 =lax.broadcasted_iota(jnp.int32, (KEY_DIM, 2 * bq), 0)
    qp = jnp.concatenate(
        [jnp.concatenate([jnp.where(row < HEAD_QK, qT, 0.0), jnp.where(row >= HEAD_QK, qT, 0.0)], axis=1),
         jnp.where(row2 < ALIBI_PARTS, 1.0, 0.0)], axis=0).astype(BF16)
    kbias = kbias_ref[0]
    ones_rows = jnp.ones((DENOM_ROWS, bk), BF16)

    s_bufs, p_bufs = (sa_ref, sb_ref), (pa_ref, pb_ref)

    def shift_of(j):
        return slope * (j * bk - q0).astype(F32)

    def visible(rel):
        if rel is None or rel <= 0:
            return (slice(0, 2 * bq),)
        return (slice(rel * bk, bq), slice(bq + rel * bk, 2 * bq))

    def step(i, par, rel, do_scores=True, do_exp=True):
        rel_c, rel_b = (None, None) if rel is None else (rel - 2, rel - 1)
        v_aug = jnp.concatenate([vT_ref[jnp.maximum(i - 2, 0)], ones_rows], axis=0)
        for c in visible(rel_c):
            acc_ref[:, c] = alpha_ref[par, :, c] * acc_ref[:, c] + jnp.dot(
                v_aug, p_bufs[par][:, c], preferred_element_type=F32)
        if do_exp:
            for c in visible(rel_b):
                p_bufs[1 - par][:, c] = jnp.exp2(
                    s_bufs[1 - par][:, c] - (m_ref[:, c] - shift_of(i - 1))).astype(BF16)
        if do_scores:
            kblk = k_ref[pl.ds(pl.multiple_of(i * bk, bk), bk), :]
            k_aug = jnp.concatenate([kblk, kbias], axis=1)
            for c in visible(rel):
                s = jnp.dot(k_aug, qp[:, c], preferred_element_type=F32)
                if rel is not None:
                    kk = lax.broadcasted_iota(jnp.int32, s.shape, 0) + rel * bk
                    qq = lax.broadcasted_iota(jnp.int32, s.shape, 1) + c.start
                    s = jnp.where(kk <= jnp.where(qq >= bq, qq - bq, qq), s, -jnp.inf)
                s_bufs[par][:, c] = s
                m_prev = m_ref[:, c]
                m_new = jnp.maximum(m_prev, jnp.max(s, axis=0, keepdims=True) + shift_of(i))
                alpha_ref[par, :, c] = jnp.exp2(m_prev - m_new)
                m_ref[:, c] = m_new

    m_ref[...] = jnp.full(m_ref.shape, NEG_BIG, F32)
    acc_ref[...] = jnp.zeros_like(acc_ref)
    alpha_ref[...] = jnp.ones_like(alpha_ref)
    pa_ref[...] = jnp.zeros_like(pa_ref)
    sb_ref[...] = jnp.full(sb_ref.shape, -jnp.inf, F32)
    n_full = qi * per_q

    def pair(t, carry):
        step(2 * t, 0, None)
        step(2 * t + 1, 1, None)
        return carry

    lax.fori_loop(0, n_full // 2, pair, 0)
    for d in range(per_q + 2):
        step(n_full + d, d % 2, d, do_scores=d < per_q, do_exp=d <= per_q)

    acc = acc_ref[...]
    o_all = acc[:HEAD_V] / acc[HEAD_V:HEAD_V + 1]
    lam = _diff_lambda(lamv_ref[...], lam0)
    o = o_all[:, :bq] - lam * o_all[:, bq:]
    ms = jnp.mean(o * o, axis=0, keepdims=True)
    o = o * lax.rsqrt(ms + RMS_EPS) * sg_ref[...] * (1.0 - lam0)
    o_ref[...] = o.T.astype(BF16)


def _attn_prompt_call(slopes_l2, q, kb, vT, kbias, lamv, sg, lam0):
    t = q.shape[0]
    bq, bk = ATTN_BQ, ATTN_BK
    assert bq % (2 * bk) == 0
    nk = t // bk
    acc_rows = HEAD_V + DENOM_ROWS
    return pl.pallas_call(
        functools.partial(_attn_prompt_kernel, bq=bq, bk=bk, lam0=lam0),
        grid=(N_HEADS, t // bq),
        in_specs=[pl.BlockSpec(memory_space=pltpu.SMEM),
                  pl.BlockSpec((bq, KEY_DIM), lambda h, i: (i, h)),
                  pl.BlockSpec((t, KEY_DIM), lambda h, i: (0, h)),
                  pl.BlockSpec((nk, HEAD_V, bk), lambda h, i: (0, h, 0)),
                  pl.BlockSpec((1, bk, KEY_DIM), lambda h, i: (h, 0, 0)),
                  pl.BlockSpec(lamv.shape, lambda h, i: (0, 0)),
                  pl.BlockSpec(sg.shape, lambda h, i: (0, 0))],
        out_specs=pl.BlockSpec((bq, HEAD_V), lambda h, i: (i, h)),
        out_shape=jax.ShapeDtypeStruct((t, D_MODEL), BF16),
        scratch_shapes=[pltpu.VMEM((bk, 2 * bq), F32), pltpu.VMEM((bk, 2 * bq), F32),
                        pltpu.VMEM((bk, 2 * bq), BF16), pltpu.VMEM((bk, 2 * bq), BF16),
                        pltpu.VMEM((1, 2 * bq), F32), pltpu.VMEM((2, 1, 2 * bq), F32),
                        pltpu.VMEM((acc_rows, 2 * bq), F32)],
        compiler_params=_cparams("arbitrary", "arbitrary"),
        name="diff_attn_prompt",
    )(slopes_l2, q, kb, vT, kbias, lamv, sg)


def _attn_sample_kernel(pt_ref, *refs, n_pages, page, ds, lam0):
    k_refs = refs[:n_pages]
    v_refs = refs[n_pages:2 * n_pages]
    (qbd_ref, bias_ref, slope_ref, knew_ref, vnew_ref, bnew_ref, lamv_ref, sg_ref,
     o_ref, m_ref, l_ref, acc_ref) = refs[2 * n_pages:]
    c = pl.program_id(1)
    nc = pl.num_programs(1)
    chunk = n_pages * page

    @pl.when(c == 0)
    def _():
        m_ref[...] = jnp.full(m_ref.shape, NEG_BIG, F32)
        l_ref[...] = jnp.zeros_like(l_ref)
        acc_ref[...] = jnp.zeros_like(acc_ref)

    def update(s, shift, v_bf16):
        m_old = m_ref[...]
        m_new = jnp.maximum(m_old, jnp.max(s, axis=1, keepdims=True) + shift)
        alpha = jnp.exp2(m_old - m_new)
        p = jnp.exp2(s - (m_new - shift))
        l_ref[...] = alpha * l_ref[...] + jnp.sum(p, axis=1, keepdims=True)
        acc_ref[...] = alpha * acc_ref[...] + jnp.dot(p.astype(BF16), v_bf16,
                                                      preferred_element_type=F32)
        m_ref[...] = m_new

    def head_major(refs):
        return jnp.concatenate(
            [jnp.concatenate([r[pl.ds(hd, page, stride=N_HEADS), :].astype(BF16) for r in refs], axis=0)
             for hd in range(N_HEADS)], axis=1)

    sT = jnp.dot(head_major(k_refs), qbd_ref[0], preferred_element_type=F32)
    s = sT.T + bias_ref[...]
    past = nc * chunk
    shift = slope_ref[...] * (c * chunk - past).astype(F32)
    update(s, shift, head_major(v_refs))

    @pl.when(c == nc - 1)
    def _():
        sTn = jnp.dot(knew_ref[0], qbd_ref[0], preferred_element_type=F32)
        sn = sTn.T + bnew_ref[...]
        update(sn, jnp.zeros_like(slope_ref[...]), vnew_ref[0])

        lam = _diff_lambda(lamv_ref[...], lam0)
        acc = acc_ref[...]
        l = l_ref[...]
        rows = 2 * ds
        for hd in range(N_HEADS):
            blk = acc[hd * rows:(hd + 1) * rows, hd * HEAD_V:(hd + 1) * HEAD_V] / l[hd * rows:(hd + 1) * rows]
            o = blk[0:ds] - lam * blk[ds:rows]
            ms = jnp.mean(o * o, axis=1, keepdims=True)
            o = o * lax.rsqrt(ms + RMS_EPS) * sg_ref[...] * (1.0 - lam0)
            o_ref[0, 0:ds, hd * HEAD_V:(hd + 1) * HEAD_V] = o
            o_ref[0, ds:rows, hd * HEAD_V:(hd + 1) * HEAD_V] = jnp.zeros((rows - ds, HEAD_V), F32)


def _attn_sample_call(page_table, cache_k, cache_v, qbd, bias, slope_col, knew, vnew, bnew, lamv, sg_row,
                      lam0, ds):
    nb, n_pages_total = page_table.shape
    page = cache_k.shape[1] // N_HEADS
    n_pages = PAGES_PER_STEP
    chunk = n_pages * page
    rows = 2 * ds
    page_spec = lambda i: pl.BlockSpec((None, page * N_HEADS, KEY_DIM),
                                       lambda b, c, pt, i=i: (pt[b, c * n_pages + i], 0, 0))
    const2 = lambda a: pl.BlockSpec(a.shape, lambda b, c, pt: (0, 0))
    per_seq = lambda a: pl.BlockSpec((1,) + a.shape[1:], lambda b, c, pt: (b, 0, 0))
    grid_spec = pltpu.PrefetchScalarGridSpec(
        num_scalar_prefetch=1,
        grid=(nb, n_pages_total // n_pages),
        in_specs=([page_spec(i) for i in range(n_pages)] + [page_spec(i) for i in range(n_pages)]
                  + [per_seq(qbd), const2(bias), const2(slope_col), per_seq(knew), per_seq(vnew),
                     const2(bnew), const2(lamv), const2(sg_row)]),
        out_specs=pl.BlockSpec((1, rows, D_MODEL), lambda b, c, pt: (b, 0, 0)),
        scratch_shapes=[pltpu.VMEM((LANES, 1), F32), pltpu.VMEM((LANES, 1), F32),
                        pltpu.VMEM((LANES, D_MODEL), F32)])
    return pl.pallas_call(
        functools.partial(_attn_sample_kernel, n_pages=n_pages, page=page, ds=ds, lam0=lam0),
        grid_spec=grid_spec,
        out_shape=jax.ShapeDtypeStruct((nb, rows, D_MODEL), F32),
        compiler_params=_cparams("arbitrary", "arbitrary"),
        name="diff_attn_sample",
    )(page_table, *([cache_k] * n_pages), *([cache_v] * n_pages), qbd, bias, slope_col, knew, vnew,
      bnew, lamv, sg_row)


def _oproj_kernel(o_ref, x_ref, mod_ref, wo_ref, gffn_ref, wrT_ref, rb_ref, x1_ref, h2_ref, gT_ref):
    mod6 = _split_mod(mod_ref[...])
    y = jnp.dot(o_ref[...], wo_ref[...], preferred_element_type=F32)
    _ffn_prologue(x_ref[...] + mod6[2] * y, mod6, gffn_ref, wrT_ref, rb_ref, x1_ref, h2_ref, gT_ref)


def _oproj_call(o, x, mod, wo, gffn, wrT, rb):
    t = x.shape[0]
    tm = min(ROW_TILE, t)
    const = lambda shape: pl.BlockSpec(shape, lambda i: (0,) * len(shape))
    row = lambda: pl.BlockSpec((tm, D_MODEL), lambda i: (i, 0))
    mod_spec = const((1, 6 * D_MODEL)) if mod.shape[0] == 1 else pl.BlockSpec((tm, 6 * D_MODEL), lambda i: (i, 0))
    return pl.pallas_call(
        _oproj_kernel,
        grid=(t // tm,),
        in_specs=[row(), row(), mod_spec, const(wo.shape), const((1, D_MODEL)),
                  const((N_EXPERTS, D_MODEL)), const((N_EXPERTS, 1))],
        out_specs=[row(), row(), pl.BlockSpec((N_EXPERTS, tm), lambda i: (0, i))],
        out_shape=[jax.ShapeDtypeStruct((t, D_MODEL), F32),
                   jax.ShapeDtypeStruct((t, D_MODEL), BF16),
                   jax.ShapeDtypeStruct((N_EXPERTS, t), F32)],
        compiler_params=_cparams("arbitrary"),
        name="attn_out_proj",
    )(o, x, mod, wo, gffn, wrT, rb)


def kernel(x_prompt, x_sample, state_pool, cache_k, cache_v, page_table, c_prompt, c_sample,
           w_mod, b_mod, norm_mix_g, norm_ffn_g, pool_w, pool_scale, w_qkv, q_norm_g, k_norm_g,
           lambda_q1, lambda_k1, lambda_q2, lambda_k2, subln_g, w_o, w_router, router_bias,
           w_gate, w_up, w_down):
    assert DEPTH == 2 and x_prompt.shape[0] == 1
    seq = x_prompt.shape[1]
    nb, ds, _ = x_sample.shape
    n_phys, page = cache_k.shape[1], cache_k.shape[2]
    past = page_table.shape[1] * page
    ts = nb * ds

    c_all = jnp.concatenate([c_prompt, jnp.zeros((7, D_MODEL), F32), c_sample], axis=0)
    mods = _mod_call(c_all, w_mod, b_mod)
    mod_p = mods[:, 0:1]
    mod_s = jnp.tile(mods[:, 8:8 + nb], (1, ds, 1))

    pool_w_b = pool_w.astype(BF16)
    wgu = jnp.concatenate([w_gate, w_up], axis=-1).astype(BF16)
    wd = w_down.astype(BF16)
    wqkv = w_qkv.astype(BF16)
    wo = w_o.astype(BF16)
    wrT = w_router.T
    rb = router_bias.reshape(N_EXPERTS, 1)
    gq = jnp.tile(q_norm_g.reshape(1, 1, KEY_DIM), (1, N_HEADS, 1)).reshape(-1, 1, D_MODEL)
    gk = jnp.tile(k_norm_g.reshape(1, 1, KEY_DIM), (1, N_HEADS, 1)).reshape(-1, 1, D_MODEL)
    seg_id = jnp.arange(D_MODEL) // HEAD_QK
    seg = (seg_id[:, None] == jnp.arange(LANES)[None, :]).astype(BF16)
    segT = seg.T
    slopes = 2.0 ** (-8.0 * jnp.arange(1, N_HEADS + 1, dtype=F32) / N_HEADS)
    slopes_l2 = slopes * LOG2E

    xs = x_sample.transpose(1, 0, 2).reshape(ts, D_MODEL)
    state = state_pool.transpose(0, 2, 1, 3)

    x1p, h2p, gTp, pst = _pool_prompt_call(x_prompt[0], mod_p[0], norm_mix_g[0:1], norm_ffn_g[0:1],
                                           pool_w_b[0], pool_scale[0:1], wrT, rb)
    x1s, h2s, gTs, pool_s = _pool_sample_call(xs, state[0], mod_s[0], norm_mix_g[0:1], norm_ffn_g[0:1],
                                              pool_w[0], pool_scale[0:1], wrT, rb, nb, ds)
    xp = _moe_routed_call(h2p, gTp, x1p, mod_p[0][:, 5 * D_MODEL:], wgu[0], wd[0])
    xs = _moe_call(h2s, gTs.T, x1s, mod_s[0][:, 5 * D_MODEL:], wgu[0], wd[0])

    lam0 = _lambda_init(1)
    lamv = jnp.stack([lambda_q1[0], lambda_k1[0], lambda_q2[0], lambda_k2[0]])
    qb, k32, v32, kb, vT = _qkv_call(xp, mod_p[1], norm_mix_g[1:2], wqkv[0], gq[0], gk[0], seg, segT, True)
    rest = slopes_l2[:, None] * jnp.arange(ATTN_BK, dtype=F32)[None, :]
    parts = []
    for _ in range(ALIBI_PARTS):
        part = _truncate_to_bf16(rest)
        parts.append(part)
        rest = rest - part
    kbias = jnp.pad(jnp.stack(parts, axis=-1).astype(BF16), ((0, 0), (0, 0), (0, KEY_DIM - ALIBI_PARTS)))
    o_p = _attn_prompt_call(slopes_l2, qb, kb, vT, kbias, lamv, subln_g[0].reshape(HEAD_V, 1), lam0)
    x1p, h2p, gTp = _oproj_call(o_p, xp, mod_p[1], wo[0], norm_ffn_g[1:2], wrT, rb)
    yp = _moe_routed_call(h2p, gTp, x1p, mod_p[1][:, 5 * D_MODEL:], wgu[1], wd[1])

    qs, ks32, vs32 = _qkv_call(xs, mod_s[1], norm_mix_g[1:2], wqkv[0], gq[0], gk[0], seg, segT, False)
    rows = 2 * ds
    q_seq = qs.astype(F32).reshape(ds, nb, D_MODEL).transpose(1, 2, 0)
    colmask = (jnp.arange(D_MODEL)[:, None] // HEAD_QK) == (jnp.arange(N_HEADS * rows)[None, :] // ds)
    qbd = jnp.where(colmask[None], jnp.tile(q_seq, (1, 1, 2 * N_HEADS)), 0.0)
    qbd = jnp.pad(qbd, ((0, 0), (0, 0), (0, LANES - N_HEADS * rows))).astype(BF16)
    row_slope = jnp.pad(jnp.repeat(slopes_l2, rows), (0, LANES - N_HEADS * rows))
    chunk = PAGES_PER_STEP * page
    bias_s = row_slope[:, None] * jnp.arange(chunk, dtype=F32)[None, :]
    jn = jnp.arange(LANES)[None, :]
    tok = (jnp.arange(LANES) % ds)[:, None]
    bias_new = jnp.where(jnp.logical_and(jn <= tok, jn < ds), row_slope[:, None] * jn.astype(F32), -jnp.inf)
    to_seq = lambda a: jnp.pad(a.reshape(ds, nb, D_MODEL).transpose(1, 0, 2),
                               ((0, 0), (0, LANES - ds), (0, 0))).astype(BF16)
    o_s = _attn_sample_call(page_table, cache_k.reshape(-1, page * N_HEADS, KEY_DIM),
                            cache_v.reshape(-1, page * N_HEADS, HEAD_V), qbd, bias_s,
                            row_slope.reshape(LANES, 1), to_seq(ks32), to_seq(vs32), bias_new, lamv,
                            subln_g[0].reshape(1, HEAD_V), lam0, ds)
    o_s = o_s[:, :ds].transpose(1, 0, 2).reshape(ts, D_MODEL).astype(BF16)
    x1s, h2s, gTs = _oproj_call(o_s, xs, mod_s[1], wo[0], norm_ffn_g[1:2], wrT, rb)
    ys = _moe_call(h2s, gTs.T, x1s, mod_s[1][:, 5 * D_MODEL:], wgu[1], wd[1])

    seq_major = lambda a: a.reshape(ds, nb, D_MODEL).transpose(1, 0, 2)
    return (yp[None],
            seq_major(ys),
            pst[None, None, 1:],
            pool_s.transpose(1, 0, 2)[None],
            k32.reshape(1, 1, seq // page, page, N_HEADS, KEY_DIM),
            v32.reshape(1, 1, seq // page, page, N_HEADS, HEAD_V),
            seq_major(ks32).reshape(1, nb, ds, N_HEADS, KEY_DIM),
            seq_major(vs32).reshape(1, nb, ds, N_HEADS, HEAD_V))
```
